```python
import math
import jax, jax.numpy as jnp
from jax import lax
import numpy as np

D_MODEL = 4096
BATCH = 2
SEQ = 8192
DEPTH = 4
DEC_BATCH = 2
DEC_SEQ = 4096
PAST_LEN = 128

MIXER_KINDS = ('hyena', 'pool', 'gdn')
N_MIXERS = 3
N_MOD = 6
NORM_EPS = 1e-6

HY_ORDER = 2
HY_DIRS = 2
HY_EMB = 33
HY_BANDS = (HY_EMB - 1) // 2
HY_HIDDEN = 64
HY_MIN_RATE = 3.0
HY_MAX_RATE = 15.0

POOL_WINDOWS = (2, 4, 8, 16)
POOL_GROUP = D_MODEL // len(POOL_WINDOWS)

GDN_HEADS = 32
GDN_HEAD_DIM = D_MODEL // GDN_HEADS
GDN_CHUNK = 64

PEER_HEADS = 8
PEER_N_KEYS = 128
PEER_N_EXPERTS = PEER_N_KEYS * PEER_N_KEYS
PEER_KEY_DIM = 128
PEER_TOPK = 16
PEER_BLOCK = 64

kernel_name = 'hybrid_bidir_hyena_pool_gdn_peer_encoder'


def _rmsnorm(x, g):
    xf = x.astype(jnp.float32)
    y = xf * lax.rsqrt(jnp.mean(xf * xf, axis=-1, keepdims=True) + NORM_EPS)
    return (y * g.astype(jnp.float32)).astype(x.dtype)


def _l2norm(x):
    return x * lax.rsqrt(jnp.sum(x * x, axis=-1, keepdims=True) + NORM_EPS)


def _short_conv3(x, w):
    L = x.shape[1]
    xp = jnp.pad(x, ((0, 0), (1, 1), (0, 0)))
    return xp[:, :L] * w[0] + xp[:, 1:L + 1] * w[1] + xp[:, 2:] * w[2]


def _hyena_filters(L, w1, b1, w2, b2, w3, b3, sin_freq, log_decay):
    f32 = jnp.float32
    t = jnp.linspace(0.0, 1.0, L, dtype=f32)[:, None]
    omega = 2.0 * math.pi * jnp.arange(L, dtype=f32)[:, None] / L
    bands = jnp.linspace(1e-4, HY_BANDS - 1, HY_BANDS, dtype=f32)[None, :]
    ang = omega * bands
    feats = jnp.concatenate([t, jnp.cos(ang), -jnp.sin(ang)], axis=-1)
    sf = sin_freq.astype(f32)
    h = jnp.sin(sf[0] * (feats @ w1.astype(f32) + b1.astype(f32)))
    h = jnp.sin(sf[1] * (h @ w2.astype(f32) + b2.astype(f32)))
    h = (h @ w3.astype(f32) + b3.astype(f32)).reshape(L, HY_ORDER, HY_DIRS, D_MODEL)
    window = jnp.exp(-t[:, :, None, None] * jnp.exp(log_decay.astype(f32))[None])
    return h * window


def _two_sided(h_fwd, h_bwd):
    filt = jnp.concatenate([h_fwd, jnp.zeros((1, h_fwd.shape[1]), h_fwd.dtype), h_bwd[:0:-1]], axis=0)
    return filt * lax.rsqrt(jnp.sum(filt * filt, axis=0, keepdims=True) + NORM_EPS)


def _fft_conv(z, filt):
    L = z.shape[1]
    zf = jnp.fft.rfft(z, n=2 * L, axis=1)
    ff = jnp.fft.rfft(filt, n=2 * L, axis=0)
    return jnp.fft.irfft(zf * ff[None], n=2 * L, axis=1)[:, :L]


def _hyena_mixer(h, w_in, conv, w1, b1, w2, b2, w3, b3, sin_freq, log_decay, bias, w_out):
    L = h.shape[1]
    u = _short_conv3(h @ w_in, conv).astype(jnp.float32)
    v, x1, x2 = jnp.split(u, 3, axis=-1)
    filt = _hyena_filters(L, w1, b1, w2, b2, w3, b3, sin_freq, log_decay)
    bias = bias.astype(jnp.float32)
    z = v
    for o, gate in enumerate((x1, x2)):
        fo = _two_sided(filt[:, o, 0], filt[:, o, 1])
        z = gate * (_fft_conv(z, fo) + z * bias[o])
    return z @ w_out


def _pool_mixer(h, w_in, w_grp, scale, w_out):
    B, L, D = h.shape
    uf = (h @ w_in).astype(jnp.float32)
    cs = jnp.concatenate([jnp.zeros((B, 1, D), jnp.float32), jnp.cumsum(uf, axis=1)], axis=1)
    pos = jnp.arange(L)
    outs = []
    for gi, w in enumerate(POOL_WINDOWS):
        lo = jnp.clip(pos - w // 2, 0, L)
        hi = jnp.clip(pos + w // 2, 0, L)
        csg = cs[:, :, gi * POOL_GROUP:(gi + 1) * POOL_GROUP]
        cnt = (hi - lo).astype(jnp.float32)[None, :, None]
        outs.append((csg[:, hi] - csg[:, lo]) / cnt - uf[:, :, gi * POOL_GROUP:(gi + 1) * POOL_GROUP])
    p = jnp.stack(outs, axis=2)
    y = jnp.einsum('blgi,gio->blgo', p, w_grp).reshape(B, L, D) * scale
    return y @ w_out


def _chunk_gated_delta(q, k, v, g, beta):
    B, L, H, dk = q.shape
    dv = v.shape[-1]
    C = GDN_CHUNK
    n = L // C

    def chunks(t):
        t = t.reshape((B, n, C, H) + t.shape[3:])
        return jnp.moveaxis(t, 3, 1)

    q, k, v, g, beta = (chunks(t) for t in (q, k, v, g, beta))
    gc = jnp.cumsum(g, axis=-1)
    incl = jnp.tril(jnp.ones((C, C), dtype=bool))
    strict = jnp.tril(jnp.ones((C, C), dtype=bool), -1)
    diff = gc[..., :, None] - gc[..., None, :]
    decay = jnp.where(incl, jnp.exp(jnp.where(incl, diff, 0.0)), 0.0)
    kb = k * beta[..., None]
    vb = v * beta[..., None]
    a_kk = jnp.where(strict, jnp.einsum('bhnid,bhnjd->bhnij', kb, k) * decay, 0.0)
    eye = jnp.eye(C, dtype=jnp.float32)
    rhs = jnp.concatenate([vb, kb * jnp.exp(gc)[..., None]], axis=-1)
    sol = lax.linalg.triangular_solve(a_kk + eye, rhs, left_side=True, lower=True, unit_diagonal=True)
    u_base, w_s = sol[..., :dv], sol[..., dv:]
    a_qk = jnp.einsum('bhnid,bhnjd->bhnij', q, k) * decay
    q_s = q * jnp.exp(gc)[..., None]
    k_tail = k * jnp.exp(gc[..., -1:] - gc)[..., None]
    c_decay = jnp.exp(gc[..., -1])

    def step(S, xs):
        ub, ws, qs, aqk, kt, cd = xs
        u = ub - jnp.einsum('bhcd,bhde->bhce', ws, S)
        o = jnp.einsum('bhcd,bhde->bhce', qs, S) + jnp.einsum('bhij,bhje->bhie', aqk, u)
        S = S * cd[..., None, None] + jnp.einsum('bhcd,bhce->bhde', kt, u)
        return S, o

    xs = tuple(jnp.moveaxis(t, 2, 0) for t in (u_base, w_s, q_s, a_qk, k_tail, c_decay))
    S0 = jnp.zeros((B, H, dk, dv), jnp.float32)
    _, o = lax.scan(step, S0, xs)
    return jnp.transpose(o, (1, 0, 3, 2, 4)).reshape(B, L, H, dv)


def _gdn_mixer(h, w_in, conv, A_log, dt_bias, o_norm, w_out):
    B, L, _ = h.shape
    D, H, dh = D_MODEL, GDN_HEADS, GDN_HEAD_DIM
    f32 = jnp.float32
    proj = (h @ w_in).astype(f32)
    qkv = jax.nn.silu(_short_conv3(proj[..., :3 * D], conv.astype(f32)))
    q, k, v = (t.reshape(B, L, H, dh) for t in jnp.split(qkv, 3, axis=-1))
    q = _l2norm(q) * dh ** -0.5
    k = _l2norm(k)
    gate = proj[..., 3 * D:4 * D].reshape(B, L, H, dh)
    b_f, b_b, a_f, a_b = jnp.split(proj[..., 4 * D:], 4, axis=-1)
    A = jnp.exp(A_log.astype(f32))
    dtb = dt_bias.astype(f32)
    g_f = -A[0] * jax.nn.softplus(a_f + dtb[0])
    g_b = -A[1] * jax.nn.softplus(a_b + dtb[1])
    o_f = _chunk_gated_delta(q, k, v, g_f, jax.nn.sigmoid(b_f))
    flip = lambda t: jnp.flip(t, axis=1)
    o_b = flip(_chunk_gated_delta(flip(q), flip(k), flip(v), flip(g_b), flip(jax.nn.sigmoid(b_b))))
    o = _rmsnorm(o_f + o_b, o_norm) * jax.nn.silu(gate)
    return o.reshape(B, L, D) @ w_out


def _peer(h, w_q, keys, u_tab, v_tab):
    B, L, D = h.shape
    T = B * L
    f32 = jnp.float32
    xt = h.reshape(T, D)
    q = (xt @ w_q).reshape(T, PEER_HEADS, 2, PEER_KEY_DIM)
    s = jnp.einsum('thpd,pnd->thpn', q, keys).astype(f32)
    s_top, i_top = lax.top_k(s, PEER_TOPK)
    cand = (s_top[:, :, 0, :, None] + s_top[:, :, 1, None, :]).reshape(T, PEER_HEADS, PEER_TOPK * PEER_TOPK)
    cid = (i_top[:, :, 0, :, None] * PEER_N_KEYS + i_top[:, :, 1, None, :]).reshape(T, PEER_HEADS, PEER_TOPK * PEER_TOPK)
    best, pos = lax.top_k(cand, PEER_TOPK)
    ids = jnp.take_along_axis(cid, pos, axis=-1)
    gates = jax.nn.softmax(best, axis=-1)
    nb = T // PEER_BLOCK

    def block(args):
        xb, idb, gb = args
        act = jax.nn.gelu(jnp.einsum('thkd,td->thk', u_tab[idb], xb).astype(f32), approximate=False)
        return jnp.einsum('thk,thkd->td', (gb * act).astype(v_tab.dtype), v_tab[idb])

    out = lax.map(block, (xt.reshape(nb, PEER_BLOCK, D),
                          ids.reshape(nb, PEER_BLOCK, PEER_HEADS, PEER_TOPK),
                          gates.reshape(nb, PEER_BLOCK, PEER_HEADS, PEER_TOPK)))
    return out.reshape(B, L, D)


def _mix(kind, h, params):
    if kind == 'hyena':
        return _hyena_mixer(h, *params)
    if kind == 'pool':
        return _pool_mixer(h, *params)
    return _gdn_mixer(h, *params)


def _layer(x, c, kind, ada_w, ada_b, norm_tok, norm_ch, mixer_params, peer_params):
    mod = (jax.nn.silu(c) @ ada_w + ada_b)[:, None, :]
    sh_t, sc_t, g_t, sh_c, sc_c, g_c = jnp.split(mod, N_MOD, axis=-1)
    h = _rmsnorm(x, norm_tok) * (1.0 + sc_t) + sh_t
    x = x + (g_t * _mix(kind, h, mixer_params)).astype(x.dtype)
    h = _rmsnorm(x, norm_ch) * (1.0 + sc_c) + sh_c
    x = x + (g_c * _peer(h, *peer_params)).astype(x.dtype)
    return x


def setup_inputs(seed: int = 0) -> dict:
    key = jax.random.key(seed)
    keys = iter(jax.random.split(key, 160))
    f32 = jnp.float32
    D = D_MODEL

    def nrm(shape, scale):
        return jax.random.normal(next(keys), shape, f32) * scale

    def unif(shape, lo, hi):
        return jax.random.uniform(next(keys), shape, f32, lo, hi)

    inp = {
        'x_prompt': nrm((BATCH, SEQ, D), 1.0),
        'x_sample': nrm((DEC_BATCH, DEC_SEQ, D), 1.0),
        'c_prompt': nrm((BATCH, D), 1.0),
        'c_sample': nrm((DEC_BATCH, D), 1.0),
    }
    for i in range(DEPTH):
        p = 'l%d_' % i
        kind = MIXER_KINDS[i % N_MIXERS]
        inp[p + 'ada_w'] = nrm((D, N_MOD * D), 0.5 * D ** -0.5)
        inp[p + 'ada_b'] = nrm((N_MOD * D,), 0.02)
        inp[p + 'norm_tok'] = 1.0 + nrm((D,), 0.1)
        inp[p + 'norm_ch'] = 1.0 + nrm((D,), 0.1)
        if kind == 'hyena':
            inp[p + 'hy_w_in'] = nrm((D, 3 * D), D ** -0.5)
            inp[p + 'hy_conv'] = nrm((3, 3 * D), 3 ** -0.5)
            inp[p + 'hy_ffn_w1'] = nrm((HY_EMB, HY_HIDDEN), HY_EMB ** -0.5)
            inp[p + 'hy_ffn_b1'] = nrm((HY_HIDDEN,), 0.1)
            inp[p + 'hy_ffn_w2'] = nrm((HY_HIDDEN, HY_HIDDEN), HY_HIDDEN ** -0.5)
            inp[p + 'hy_ffn_b2'] = nrm((HY_HIDDEN,), 0.1)
            inp[p + 'hy_ffn_w3'] = nrm((HY_HIDDEN, HY_ORDER * HY_DIRS * D), HY_HIDDEN ** -0.5)
            inp[p + 'hy_ffn_b3'] = nrm((HY_ORDER * HY_DIRS * D,), 0.1)
            inp[p + 'hy_sin_freq'] = 1.0 + nrm((2, HY_HIDDEN), 0.1)
            inp[p + 'hy_log_decay'] = unif((HY_ORDER, HY_DIRS, D), math.log(HY_MIN_RATE), math.log(HY_MAX_RATE))
            inp[p + 'hy_bias'] = nrm((HY_ORDER, D), 0.5)
            inp[p + 'hy_w_out'] = nrm((D, D), D ** -0.5)
        elif kind == 'pool':
            inp[p + 'pool_w_in'] = nrm((D, D), D ** -0.5)
            inp[p + 'pool_w_grp'] = nrm((len(POOL_WINDOWS), POOL_GROUP, POOL_GROUP), POOL_GROUP ** -0.5)
            inp[p + 'pool_scale'] = 1.0 + nrm((D,), 0.1)
            inp[p + 'pool_w_out'] = nrm((D, D), D ** -0.5)
        else:
            inp[p + 'gdn_w_in'] = nrm((D, 4 * D + 4 * GDN_HEADS), D ** -0.5)
            inp[p + 'gdn_conv'] = nrm((3, 3 * D), 3 ** -0.5)
            inp[p + 'gdn_A_log'] = jnp.log(unif((2, GDN_HEADS), 1.0, 16.0))
            dt = jnp.exp(unif((2, GDN_HEADS), math.log(1e-3), math.log(1e-1)))
            inp[p + 'gdn_dt_bias'] = dt + jnp.log(-jnp.expm1(-dt))
            inp[p + 'gdn_o_norm'] = 1.0 + nrm((GDN_HEAD_DIM,), 0.1)
            inp[p + 'gdn_w_out'] = nrm((D, D), D ** -0.5)
        inp[p + 'peer_w_q'] = nrm((D, PEER_HEADS * 2 * PEER_KEY_DIM), D ** -0.5)
        inp[p + 'peer_keys'] = nrm((2, PEER_N_KEYS, PEER_KEY_DIM), PEER_KEY_DIM ** -0.5)
        inp[p + 'peer_u'] = nrm((PEER_N_EXPERTS, D), D ** -0.5)
        inp[p + 'peer_v'] = nrm((PEER_N_EXPERTS, D), (PEER_HEADS * PEER_TOPK) ** -0.5)
    inp['final_norm'] = 1.0 + nrm((D,), 0.1)
    return inp


def reference(x_prompt, x_sample, c_prompt, c_sample,
              l0_ada_w, l0_ada_b, l0_norm_tok, l0_norm_ch,
              l0_hy_w_in, l0_hy_conv, l0_hy_ffn_w1, l0_hy_ffn_b1, l0_hy_ffn_w2, l0_hy_ffn_b2,
              l0_hy_ffn_w3, l0_hy_ffn_b3, l0_hy_sin_freq, l0_hy_log_decay, l0_hy_bias, l0_hy_w_out,
              l0_peer_w_q, l0_peer_keys, l0_peer_u, l0_peer_v,
              l1_ada_w, l1_ada_b, l1_norm_tok, l1_norm_ch,
              l1_pool_w_in, l1_pool_w_grp, l1_pool_scale, l1_pool_w_out,
              l1_peer_w_q, l1_peer_keys, l1_peer_u, l1_peer_v,
              l2_ada_w, l2_ada_b, l2_norm_tok, l2_norm_ch,
              l2_gdn_w_in, l2_gdn_conv, l2_gdn_A_log, l2_gdn_dt_bias, l2_gdn_o_norm, l2_gdn_w_out,
              l2_peer_w_q, l2_peer_keys, l2_peer_u, l2_peer_v,
              l3_ada_w, l3_ada_b, l3_norm_tok, l3_norm_ch,
              l3_hy_w_in, l3_hy_conv, l3_hy_ffn_w1, l3_hy_ffn_b1, l3_hy_ffn_w2, l3_hy_ffn_b2,
              l3_hy_ffn_w3, l3_hy_ffn_b3, l3_hy_sin_freq, l3_hy_log_decay, l3_hy_bias, l3_hy_w_out,
              l3_peer_w_q, l3_peer_keys, l3_peer_u, l3_peer_v,
              final_norm):
    layers = (
        (l0_ada_w, l0_ada_b, l0_norm_tok, l0_norm_ch,
         (l0_hy_w_in, l0_hy_conv, l0_hy_ffn_w1, l0_hy_ffn_b1, l0_hy_ffn_w2, l0_hy_ffn_b2,
          l0_hy_ffn_w3, l0_hy_ffn_b3, l0_hy_sin_freq, l0_hy_log_decay, l0_hy_bias, l0_hy_w_out),
         (l0_peer_w_q, l0_peer_keys, l0_peer_u, l0_peer_v)),
        (l1_ada_w, l1_ada_b, l1_norm_tok, l1_norm_ch,
         (l1_pool_w_in, l1_pool_w_grp, l1_pool_scale, l1_pool_w_out),
         (l1_peer_w_q, l1_peer_keys, l1_peer_u, l1_peer_v)),
        (l2_ada_w, l2_ada_b, l2_norm_tok, l2_norm_ch,
         (l2_gdn_w_in, l2_gdn_conv, l2_gdn_A_log, l2_gdn_dt_bias, l2_gdn_o_norm, l2_gdn_w_out),
         (l2_peer_w_q, l2_peer_keys, l2_peer_u, l2_peer_v)),
        (l3_ada_w, l3_ada_b, l3_norm_tok, l3_norm_ch,
         (l3_hy_w_in, l3_hy_conv, l3_hy_ffn_w1, l3_hy_ffn_b1, l3_hy_ffn_w2, l3_hy_ffn_b2,
          l3_hy_ffn_w3, l3_hy_ffn_b3, l3_hy_sin_freq, l3_hy_log_decay, l3_hy_bias, l3_hy_w_out),
         (l3_peer_w_q, l3_peer_keys, l3_peer_u, l3_peer_v)),
    )

    def trunk(x, c):
        for i in range(DEPTH):
            x = _layer(x, c, MIXER_KINDS[i % N_MIXERS], *layers[i])
        return _rmsnorm(x, final_norm)

    y_prompt = trunk(x_prompt, c_prompt)
    y_sample = trunk(x_sample, c_sample)
    return (y_prompt, y_sample)
```

```python
import functools
import math

import jax
import jax.numpy as jnp
from jax import lax
from jax.experimental import pallas as pl
from jax.experimental.pallas import tpu as pltpu

F32 = jnp.float32
BF16 = jnp.bfloat16
I32 = jnp.int32
HIGHEST = lax.Precision.HIGHEST

NORM_EPS = 1e-6
N_MOD = 6
LANES = 128
SUBLANES = 8
VMEM_LIMIT_BYTES = 56 * 1024 * 1024
FFT_N2 = 128
POOL_WINDOWS = (2, 4, 8, 16)
POOL_HALO = 8
GDN_CHUNK = 64
PEER_TOPK = 16
PEER_TOKENS_PER_STEP = 16
PEER_ISSUE_UNROLL = 8


def _pick(n, pref, mult=SUBLANES):
    t = (min(pref, n) // mult) * mult
    while t >= mult:
        if n % t == 0:
            return t
        t -= mult
    return n


def _params(*sem):
    return pltpu.CompilerParams(dimension_semantics=sem, vmem_limit_bytes=VMEM_LIMIT_BYTES)


def _split2(x):
    hi = x.astype(BF16)
    lo = (x - hi.astype(F32)).astype(BF16)
    return hi, lo


def _split3(x):
    hi = x.astype(BF16)
    r = x - hi.astype(F32)
    mid = r.astype(BF16)
    lo = (r - mid.astype(F32)).astype(BF16)
    return hi, mid, lo


def _dot(a, b):
    return jnp.dot(a, b, preferred_element_type=F32)


def _dot_nt(a, b):
    return lax.dot_general(a, b, (((1,), (1,)), ((), ())), preferred_element_type=F32)


def _dot_tn(a, b):
    return lax.dot_general(a, b, (((0,), (0,)), ((), ())), preferred_element_type=F32)


def _dot3(a_hi, a_lo, b_hi, b_lo, dot=_dot):
    return dot(a_hi, b_hi) + dot(a_lo, b_hi) + dot(a_hi, b_lo)


def _dot_x3(a, b, dot=_dot):
    a_hi, a_lo = _split2(a)
    b_hi, b_lo = _split2(b)
    return _dot3(a_hi, a_lo, b_hi, b_lo, dot)


def _dot_exact_lhs(a_exact, b):
    a = a_exact.astype(BF16)
    b_hi, b_mid, b_lo = _split3(b)
    return _dot(a, b_hi) + _dot(a, b_mid) + _dot(a, b_lo)


def _silu(x):
    return x * jax.nn.sigmoid(x)


def _ada_kernel(c_ref, w_ref, b_ref, o_ref):
    a = _silu(c_ref[...])
    o_ref[...] = jnp.dot(a, w_ref[...], precision=HIGHEST, preferred_element_type=F32) + b_ref[...]


def _ada(c_rows, w, b):
    rows, d = c_rows.shape
    n = w.shape[1]
    tn = _pick(n, 512, LANES)
    return pl.pallas_call(
        _ada_kernel,
        grid=(n // tn,),
        in_specs=[pl.BlockSpec((rows, d), lambda j: (0, 0)),
                  pl.BlockSpec((d, tn), lambda j: (0, j)),
                  pl.BlockSpec((1, tn), lambda j: (0, j))],
        out_specs=pl.BlockSpec((rows, tn), lambda j: (0, j)),
        out_shape=jax.ShapeDtypeStruct((rows, n), F32),
        compiler_params=_params("parallel"),
        name="ada_mod",
    )(c_rows, w, b.reshape(1, n))


def _modulated_norm(x, g, sc, sh):
    ms = jnp.mean(x * x, axis=-1, keepdims=True)
    return (x * lax.rsqrt(ms + NORM_EPS) * g) * (1.0 + sc) + sh


def _nm_kernel(x_ref, g_ref, sc_ref, sh_ref, w_ref, o_ref, h_ref):
    @pl.when(pl.program_id(2) == 0)
    def _():
        h_ref[...] = _modulated_norm(x_ref[...], g_ref[...], sc_ref[...], sh_ref[...]).astype(BF16)

    o_ref[...] = _dot(h_ref[...], w_ref[...]).astype(o_ref.dtype)


def _nm3_kernel(x_ref, g_ref, sc_ref, sh_ref, whi_ref, wlo_ref, o_ref, hhi_ref, hlo_ref):
    @pl.when(pl.program_id(2) == 0)
    def _():
        hi, lo = _split2(_modulated_norm(x_ref[...], g_ref[...], sc_ref[...], sh_ref[...]))
        hhi_ref[...] = hi
        hlo_ref[...] = lo

    o_ref[...] = _dot3(hhi_ref[...], hlo_ref[...], whi_ref[...], wlo_ref[...]).astype(o_ref.dtype)


def _norm_matmul(x, g, sc, sh, w, three_pass=False, out_dtype=F32):
    b, l, d = x.shape
    n = w.shape[1]
    tm = _pick(l, 256)
    tn = _pick(n, 512, LANES)
    x_spec = pl.BlockSpec((None, tm, d), lambda bi, i, j: (bi, i, 0))
    g_spec = pl.BlockSpec((1, d), lambda bi, i, j: (0, 0))
    m_spec = pl.BlockSpec((None, 1, d), lambda bi, i, j: (bi, 0, 0))
    w_spec = pl.BlockSpec((d, tn), lambda bi, i, j: (0, j))
    o_spec = pl.BlockSpec((None, tm, tn), lambda bi, i, j: (bi, i, j))
    if three_pass:
        w_hi, w_lo = _split2(w)
        kern, w_args, w_specs = _nm3_kernel, (w_hi, w_lo), [w_spec, w_spec]
        scratch = [pltpu.VMEM((tm, d), BF16), pltpu.VMEM((tm, d), BF16)]
    else:
        kern, w_args, w_specs = _nm_kernel, (w.astype(BF16),), [w_spec]
        scratch = [pltpu.VMEM((tm, d), BF16)]
    return pl.pallas_call(
        kern,
        grid=(b, l // tm, n // tn),
        in_specs=[x_spec, g_spec, m_spec, m_spec] + w_specs,
        out_specs=o_spec,
        out_shape=jax.ShapeDtypeStruct((b, l, n), out_dtype),
        scratch_shapes=scratch,
        compiler_params=_params("parallel", "parallel", "arbitrary"),
        name="norm_matmul3" if three_pass else "norm_matmul",
    )(x, g.reshape(1, d), sc, sh, *w_args)


def _resid_mm_kernel(z_ref, w_ref, x_ref, gt_ref, o_ref):
    o_ref[...] = x_ref[...] + gt_ref[...] * _dot(z_ref[...].astype(BF16), w_ref[...])


def _resid_matmul(z, w, x, gate):
    b, l, k = z.shape
    n = w.shape[1]
    tm = _pick(l, 512)
    tn = _pick(n, 512, LANES)
    return pl.pallas_call(
        _resid_mm_kernel,
        grid=(b, l // tm, n // tn),
        in_specs=[pl.BlockSpec((None, tm, k), lambda bi, i, j: (bi, i, 0)),
                  pl.BlockSpec((k, tn), lambda bi, i, j: (0, j)),
                  pl.BlockSpec((None, tm, tn), lambda bi, i, j: (bi, i, j)),
                  pl.BlockSpec((None, 1, tn), lambda bi, i, j: (bi, 0, j))],
        out_specs=pl.BlockSpec((None, tm, tn), lambda bi, i, j: (bi, i, j)),
        out_shape=jax.ShapeDtypeStruct((b, l, n), F32),
        compiler_params=_params("parallel", "parallel", "parallel"),
        name="resid_matmul",
    )(z, w.astype(BF16), x, gate)


def _conv3_kernel(x_ref, p_ref, n_ref, w_ref, o_ref, *, tm, gdn_d, head_scale):
    i = pl.program_id(1)
    last = pl.num_programs(1) - 1
    x = x_ref[...]
    rows = lax.broadcasted_iota(I32, (tm, 1), 0)
    prev_row = jnp.where(i > 0, p_ref[SUBLANES - 1:SUBLANES, :], 0.0)
    next_row = jnp.where(i < last, n_ref[0:1, :], 0.0)
    x_m = jnp.where(rows == 0, prev_row, pltpu.roll(x, 1, axis=0))
    x_p = jnp.where(rows == tm - 1, next_row, pltpu.roll(x, tm - 1, axis=0))
    w = w_ref[...]
    y = x_m * w[0:1, :] + x * w[1:2, :] + x_p * w[2:3, :]
    if gdn_d:
        y = _silu(y)
        sec = (pl.program_id(2) * LANES) // gdn_d
        nrm = y * lax.rsqrt(jnp.sum(y * y, axis=-1, keepdims=True) + NORM_EPS)
        nrm = nrm * jnp.where(sec == 0, head_scale, 1.0)
        y = jnp.where(sec < 2, nrm, y)
    o_ref[...] = y


def _conv3(x, w, n_cols, gdn_d=0):
    b, l, _ = x.shape
    tm = _pick(l, 512)
    tc = LANES if gdn_d else _pick(n_cols, 1024, LANES)
    r8 = tm // SUBLANES
    kern = functools.partial(_conv3_kernel, tm=tm, gdn_d=gdn_d, head_scale=float(LANES) ** -0.5)
    return pl.pallas_call(
        kern,
        grid=(b, l // tm, n_cols // tc),
        in_specs=[pl.BlockSpec((None, tm, tc), lambda bi, i, c: (bi, i, c)),
                  pl.BlockSpec((None, SUBLANES, tc), lambda bi, i, c: (bi, jnp.maximum(i * r8 - 1, 0), c)),
                  pl.BlockSpec((None, SUBLANES, tc),
                               lambda bi, i, c: (bi, jnp.minimum((i + 1) * r8, l // SUBLANES - 1), c)),
                  pl.BlockSpec((3, tc), lambda bi, i, c: (0, c))],
        out_specs=pl.BlockSpec((None, tm, tc), lambda bi, i, c: (bi, i, c)),
        out_shape=jax.ShapeDtypeStruct((b, l, n_cols), F32),
        compiler_params=_params("parallel", "parallel", "parallel"),
        name="conv3_gdn" if gdn_d else "conv3",
    )(x, x, x, w)


def _hyfilt_kernel(band_ref, w1_ref, b1_ref, w2_ref, b2_ref, sf_ref, w3_ref, b3_ref, ld_ref,
                   o_ref, ss_ref, *, tl, seq, n_band, d_model):
    j = pl.program_id(0)
    i = pl.program_id(1)
    pos = (i * tl + lax.broadcasted_iota(I32, (tl, 1), 0)).astype(F32)
    t = pos / float(seq - 1)
    omega = (2.0 * math.pi) * pos / float(seq)
    ang = omega * band_ref[...]
    lane = lax.broadcasted_iota(I32, (tl, LANES), 1)
    feats = jnp.where(lane == 0, t,
                      jnp.where(lane <= n_band, jnp.cos(ang),
                                jnp.where(lane <= 2 * n_band, -jnp.sin(ang), 0.0)))
    sf = sf_ref[...]
    h = jnp.sin(sf[0:1, :] * (jnp.dot(feats, w1_ref[...], precision=HIGHEST) + b1_ref[...]))
    h = jnp.sin(sf[1:2, :] * (jnp.dot(h, w2_ref[...], precision=HIGHEST) + b2_ref[...]))
    h = jnp.dot(h, w3_ref[...], precision=HIGHEST) + b3_ref[...]
    out = h * jnp.exp(-t * jnp.exp(ld_ref[...]))
    o_ref[...] = out

    tn = out.shape[1]
    is_bwd = ((j * tn) // d_model) % 2 == 1
    sq = jnp.where(jnp.logical_and(is_bwd, pos == 0.0), 0.0, out * out)

    @pl.when(i == 0)
    def _():
        ss_ref[...] = jnp.zeros_like(ss_ref)

    ss_ref[...] += jnp.sum(sq, axis=0, keepdims=True)


def _hyena_filters(seq, w1, b1, w2, b2, w3, b3, sin_freq, log_decay):
    emb, hid = w1.shape
    n_band = (emb - 1) // 2
    n_out = w3.shape[1]
    d_model = log_decay.shape[-1]
    bands = jnp.linspace(1e-4, n_band - 1, n_band, dtype=F32)
    band_row = jnp.zeros((1, LANES), F32).at[0, 1:1 + n_band].set(bands).at[0, 1 + n_band:1 + 2 * n_band].set(bands)
    w1p = jnp.zeros((LANES, hid), F32).at[:emb].set(w1)
    tl = _pick(seq, 512)
    tn = _pick(d_model, 1024, LANES)
    kern = functools.partial(_hyfilt_kernel, tl=tl, seq=seq, n_band=n_band, d_model=d_model)
    full = lambda shape: pl.BlockSpec(shape, lambda j, i: (0,) * len(shape))
    return pl.pallas_call(
        kern,
        grid=(n_out // tn, seq // tl),
        in_specs=[full((1, LANES)), full((LANES, hid)), full((1, hid)), full((hid, hid)), full((1, hid)),
                  full((2, hid)),
                  pl.BlockSpec((hid, tn), lambda j, i: (0, j)),
                  pl.BlockSpec((1, tn), lambda j, i: (0, j)),
                  pl.BlockSpec((1, tn), lambda j, i: (0, j))],
        out_specs=[pl.BlockSpec((tl, tn), lambda j, i: (i, j)),
                   pl.BlockSpec((1, tn), lambda j, i: (0, j))],
        out_shape=[jax.ShapeDtypeStruct((seq, n_out), F32), jax.ShapeDtypeStruct((1, n_out), F32)],
        compiler_params=_params("parallel", "arbitrary"),
        name="hyena_filters",
    )(band_row, w1p, b1.reshape(1, hid), w2, b2.reshape(1, hid), sin_freq, w3, b3.reshape(1, n_out),
      log_decay.reshape(1, n_out))


def _lmm_kernel(mh_ref, ml_ref, x_ref, o_ref, *, gb):
    for g in range(gb):
        x_hi, x_lo = _split2(x_ref[g])
        o_ref[g] = _dot3(mh_ref[g], ml_ref[g], x_hi, x_lo)


def _lmm_epi_kernel(mh_ref, ml_ref, x_ref, z_ref, gt_ref, bias_ref, o_ref, *, gb):
    for g in range(gb):
        x_hi, x_lo = _split2(x_ref[g])
        conv = _dot3(mh_ref[g], ml_ref[g], x_hi, x_lo)
        z = z_ref[g]
        o_ref[g] = gt_ref[g] * (conv + z * bias_ref[...])


def _fft_stage1(mats, x, epilogue=None):
    m_hi, m_lo = mats
    g_n, r_out, r_in = m_hi.shape
    d = x.shape[-1]
    gb = _pick(g_n, 4, 1)
    td = _pick(d, 1024, LANES)
    m_spec = pl.BlockSpec((gb, r_out, r_in), lambda g, j: (g, 0, 0))
    x_spec = pl.BlockSpec((gb, r_in, td), lambda g, j: (g, 0, j))
    o_spec = pl.BlockSpec((gb, r_out, td), lambda g, j: (g, 0, j))
    if epilogue is None:
        kern, extra, extra_specs = functools.partial(_lmm_kernel, gb=gb), (), []
    else:
        kern = functools.partial(_lmm_epi_kernel, gb=gb)
        extra = epilogue
        extra_specs = [o_spec, o_spec, pl.BlockSpec((1, td), lambda g, j: (0, j))]
    return pl.pallas_call(
        kern,
        grid=(g_n // gb, d // td),
        in_specs=[m_spec, m_spec, x_spec] + extra_specs,
        out_specs=o_spec,
        out_shape=jax.ShapeDtypeStruct((g_n, r_out, d), F32),
        compiler_params=_params("parallel", "parallel"),
        name="fft_stage1" if epilogue is None else "fft_stage1_inv",
    )(m_hi, m_lo, x, *extra)


def _s2f_kernel(mh_ref, ml_ref, x_ref, o_ref, *, gb):
    for g in range(gb):
        x_hi, x_lo = _split2(x_ref[g])
        o_ref[g] = _dot3(mh_ref[...], ml_ref[...], x_hi, x_lo)


def _s2c_kernel(mh_ref, ml_ref, ih_ref, il_ref, x_ref, f_ref, ss_ref, o_ref, *, gb, half):
    scale = lax.rsqrt(ss_ref[...] + NORM_EPS)
    for g in range(gb):
        x_hi, x_lo = _split2(x_ref[g])
        spec = _dot3(mh_ref[...], ml_ref[...], x_hi, x_lo)
        f = f_ref[g]
        s_re, s_im = spec[:half], spec[half:]
        f_re, f_im = f[:half], f[half:]
        prod = jnp.concatenate([s_re * f_re - s_im * f_im, s_re * f_im + s_im * f_re], axis=0) * scale
        p_hi, p_lo = _split2(prod)
        o_ref[g] = _dot3(ih_ref[...], il_ref[...], p_hi, p_lo)


def _fft_stage2_fwd(mats, x):
    m_hi, m_lo = mats
    r = m_hi.shape[0]
    g_n, _, d = x.shape
    gb = _pick(g_n, 4, 1)
    td = _pick(d, 1024, LANES)
    m_spec = pl.BlockSpec((r, r), lambda g, j: (0, 0))
    x_spec = pl.BlockSpec((gb, r, td), lambda g, j: (g, 0, j))
    return pl.pallas_call(
        functools.partial(_s2f_kernel, gb=gb),
        grid=(g_n // gb, d // td),
        in_specs=[m_spec, m_spec, x_spec],
        out_specs=x_spec,
        out_shape=jax.ShapeDtypeStruct(x.shape, F32),
        compiler_params=_params("parallel", "parallel"),
        name="fft_stage2_fwd",
    )(m_hi, m_lo, x)


def _fft_stage2_conv(mats, imats, x, filt_spec, sumsq):
    m_hi, m_lo = mats
    i_hi, i_lo = imats
    r = m_hi.shape[0]
    g_n, _, d = x.shape
    gb = _pick(g_n, 4, 1)
    td = _pick(d, 1024, LANES)
    m_spec = pl.BlockSpec((r, r), lambda g, j: (0, 0))
    x_spec = pl.BlockSpec((gb, r, td), lambda g, j: (g, 0, j))
    return pl.pallas_call(
        functools.partial(_s2c_kernel, gb=gb, half=r // 2),
        grid=(g_n // gb, d // td),
        in_specs=[m_spec, m_spec, m_spec, m_spec, x_spec, x_spec, pl.BlockSpec((1, td), lambda g, j: (0, j))],
        out_specs=x_spec,
        out_shape=jax.ShapeDtypeStruct(x.shape, F32),
        compiler_params=_params("parallel", "parallel"),
        name="fft_stage2_conv",
    )(m_hi, m_lo, i_hi, i_lo, x, filt_spec, sumsq)


def _dft_tables(seq):
    n = 2 * seq
    n2 = FFT_N2
    n1 = n // n2
    n1h = n1 // 2
    k1 = jnp.arange(n1, dtype=I32)
    n2i = jnp.arange(n2, dtype=I32)
    n1i = jnp.arange(n1, dtype=I32)
    pos = n1i[None, :] * n2 + n2i[:, None]
    prod = (k1[None, :, None] * pos[:, None, :]) % n
    ang = prod.astype(F32) * (2.0 * math.pi / n)
    c, s = jnp.cos(ang), jnp.sin(ang)
    ch, sh = c[:, :, :n1h], s[:, :, :n1h]
    fwd_data = jnp.concatenate([jnp.concatenate([ch, sh], axis=2),
                                jnp.concatenate([-sh, ch], axis=2)], axis=1)
    fwd_real = jnp.concatenate([c, -s], axis=1)
    ct, st = jnp.swapaxes(ch, 1, 2), jnp.swapaxes(sh, 1, 2)
    inv_data = jnp.concatenate([jnp.concatenate([ct, -st], axis=2),
                                jnp.concatenate([st, ct], axis=2)], axis=1) * (1.0 / n)
    a2 = ((n2i[:, None] * n2i[None, :]) % n2).astype(F32) * (2.0 * math.pi / n2)
    c2, s2 = jnp.cos(a2), jnp.sin(a2)
    m2 = jnp.concatenate([jnp.concatenate([c2, s2], axis=1), jnp.concatenate([-s2, c2], axis=1)], axis=0)
    m2i = jnp.concatenate([jnp.concatenate([c2, -s2], axis=1), jnp.concatenate([s2, c2], axis=1)], axis=0)
    return dict(n1=n1, n1h=n1h, fwd_data=_split2(fwd_data), fwd_real=_split2(fwd_real),
                inv_data=_split2(inv_data), m2=_split2(m2), m2i=_split2(m2i))


def _to_stage2_layout(a, n1):
    n2, r, d = a.shape
    c = r // n1
    return a.reshape(n2, c, n1, d).transpose(2, 1, 0, 3).reshape(n1, c * n2, d)


def _to_stage1_layout(a, n2):
    n1, r, d = a.shape
    c = r // n2
    return a.reshape(n1, c, n2, d).transpose(2, 1, 0, 3).reshape(n2, c * n1, d)


def _hyena_core(u, filt, sumsq, bias):
    b, seq, d3 = u.shape
    d = d3 // 3
    tab = _dft_tables(seq)
    n1, n1h, n2 = tab["n1"], tab["n1h"], FFT_N2
    ut = u.reshape(b, n1h, n2, 3, d).transpose(3, 2, 0, 1, 4).reshape(3, n2, b * n1h, d)
    z_t = ut[0]
    filt = filt.reshape(seq, 2, 2, d)
    for o in range(2):
        h_f, h_b = filt[:, o, 0], filt[:, o, 1]
        two_sided = jnp.concatenate([h_f, jnp.zeros((1, d), F32), h_b[:0:-1]], axis=0)
        f_t = two_sided.reshape(n1, n2, d).transpose(1, 0, 2)
        f_spec = _fft_stage2_fwd(tab["m2"], _to_stage2_layout(_fft_stage1(tab["fwd_real"], f_t), n1))
        ss = sumsq.reshape(2, 2, d)[o]
        ss = (ss[0] + ss[1]).reshape(1, d)
        a = _to_stage2_layout(_fft_stage1(tab["fwd_data"], z_t), n1)
        c = _to_stage1_layout(_fft_stage2_conv(tab["m2"], tab["m2i"], a, f_spec, ss), n2)
        z_t = _fft_stage1(tab["inv_data"], c, epilogue=(z_t, ut[1 + o], bias[o].reshape(1, d)))
    return z_t.reshape(n2, b, n1h, d).transpose(1, 2, 0, 3).reshape(b, seq, d)


def _hyena_mixer(x, g, sc, sh, gate, params):
    w_in, conv, w1, b1, w2, b2, w3, b3, sin_freq, log_decay, bias, w_out = params
    seq = x.shape[1]
    y = _norm_matmul(x, g, sc, sh, w_in)
    u = _conv3(y, conv, y.shape[-1])
    filt, sumsq = _hyena_filters(seq, w1, b1, w2, b2, w3, b3, sin_freq, log_decay)
    z = _hyena_core(u, filt, sumsq, bias)
    return _resid_matmul(z, w_out, x, gate)


def _pool_kernel(x_ref, p_ref, n_ref, o_ref, *, tm, seq, group):
    i = pl.program_id(1)
    last = pl.num_programs(1) - 1
    x = x_ref[...]
    prev = jnp.where(i > 0, p_ref[...], 0.0)
    nxt = jnp.where(i < last, n_ref[...], 0.0)
    ext = jnp.concatenate([prev, x, nxt], axis=0)
    gi = (pl.program_id(2) * x.shape[1]) // group
    half = jnp.left_shift(1, gi)
    r = lax.broadcasted_iota(I32, (tm, tm + 2 * POOL_HALO), 0)
    c = lax.broadcasted_iota(I32, (tm, tm + 2 * POOL_HALO), 1)
    band = jnp.logical_and(c >= r + POOL_HALO - half, c < r + POOL_HALO + half).astype(F32)
    win = _dot_exact_lhs(band, ext)
    t = i * tm + lax.broadcasted_iota(I32, (tm, 1), 0)
    cnt = (jnp.minimum(t + half, seq) - jnp.maximum(t - half, 0)).astype(F32)
    o_ref[...] = win / cnt - x


def _pool_windows(uf):
    b, l, d = uf.shape
    group = d // len(POOL_WINDOWS)
    tm = _pick(l, 256)
    tc = _pick(group, 512, LANES)
    r8 = tm // SUBLANES
    return pl.pallas_call(
        functools.partial(_pool_kernel, tm=tm, seq=l, group=group),
        grid=(b, l // tm, d // tc),
        in_specs=[pl.BlockSpec((None, tm, tc), lambda bi, i, c: (bi, i, c)),
                  pl.BlockSpec((None, SUBLANES, tc), lambda bi, i, c: (bi, jnp.maximum(i * r8 - 1, 0), c)),
                  pl.BlockSpec((None, SUBLANES, tc),
                               lambda bi, i, c: (bi, jnp.minimum((i + 1) * r8, l // SUBLANES - 1), c))],
        out_specs=pl.BlockSpec((None, tm, tc), lambda bi, i, c: (bi, i, c)),
        out_shape=jax.ShapeDtypeStruct((b, l, d), F32),
        compiler_params=_params("parallel", "parallel", "parallel"),
        name="pool_windows",
    )(uf, uf, uf)


def _group_mm_kernel(p_ref, w_ref, s_ref, o_ref):
    o_ref[...] = (_dot(p_ref[...].astype(BF16), w_ref[...]) * s_ref[...]).astype(o_ref.dtype)


def _group_matmul(p, w_grp, scale):
    b, l, d = p.shape
    n_g, gd, _ = w_grp.shape
    tm = _pick(l, 512)
    tn = _pick(gd, 512, LANES)
    per = gd // tn
    return pl.pallas_call(
        _group_mm_kernel,
        grid=(b, l // tm, n_g, per),
        in_specs=[pl.BlockSpec((None, tm, gd), lambda bi, i, g, j: (bi, i, g)),
                  pl.BlockSpec((None, gd, tn), lambda bi, i, g, j: (g, 0, j)),
                  pl.BlockSpec((1, tn), lambda bi, i, g, j: (0, g * per + j))],
        out_specs=pl.BlockSpec((None, tm, tn), lambda bi, i, g, j: (bi, i, g * per + j)),
        out_shape=jax.ShapeDtypeStruct((b, l, d), BF16),
        compiler_params=_params("parallel", "parallel", "parallel", "parallel"),
        name="pool_group_matmul",
    )(p, w_grp.astype(BF16), scale.reshape(1, d))


def _pool_mixer(x, g, sc, sh, gate, params):
    w_in, w_grp, scale, w_out = params
    uf = _norm_matmul(x, g, sc, sh, w_in)
    y = _group_matmul(_pool_windows(uf), w_grp, scale)
    return _resid_matmul(y, w_out, x, gate)


def _gdn_gates_kernel(x_ref, alog_ref, dtb_ref, o_ref, *, n_heads):
    x = x_ref[...]
    lane = lax.broadcasted_iota(I32, x.shape, 1)
    decay = -jnp.exp(alog_ref[...]) * jax.nn.softplus(x + dtb_ref[...])
    o_ref[...] = jnp.where(lane < 2 * n_heads, jax.nn.sigmoid(x), decay)


def _gdn_gates(proj, col_block, a_log, dt_bias):
    b, l, _ = proj.shape
    n_heads = a_log.shape[1]
    pad = jnp.zeros((1, 2 * n_heads), F32)
    alog_row = jnp.concatenate([pad, a_log.reshape(1, 2 * n_heads)], axis=1)
    dtb_row = jnp.concatenate([pad, dt_bias.reshape(1, 2 * n_heads)], axis=1)
    tm = _pick(l, 1024)
    w = 4 * n_heads
    return pl.pallas_call(
        functools.partial(_gdn_gates_kernel, n_heads=n_heads),
        grid=(b, l // tm),
        in_specs=[pl.BlockSpec((None, tm, w), lambda bi, i: (bi, i, col_block)),
                  pl.BlockSpec((1, w), lambda bi, i: (0, 0)),
                  pl.BlockSpec((1, w), lambda bi, i: (0, 0))],
        out_specs=pl.BlockSpec((None, tm, w), lambda bi, i: (bi, i, 0)),
        out_shape=jax.ShapeDtypeStruct((b, l, w), F32),
        compiler_params=_params("parallel", "parallel"),
        name="gdn_gates",
    )(proj, alog_row, dtb_row)


def _unit_lower_inverse(a, row, col):
    n = a.shape[0]
    eye = (row == col).astype(F32)
    blk = lambda s: (row // s) == (col // s)
    a8 = jnp.where(blk(8), a, 0.0)
    a8_2 = _dot_x3(a8, a8)
    a8_4 = _dot_x3(a8_2, a8_2)
    t = _dot_x3(_dot_x3(eye - a8, eye + a8_2), eye + a8_4)
    s = 8
    while s < n:
        off = jnp.where(jnp.logical_and(blk(2 * s), jnp.logical_not(blk(s))), a, 0.0)
        t = t - _dot_x3(_dot_x3(t, off), t)
        s *= 2
    return t


def _gdn_scan_kernel(q_ref, k_ref, v_ref, gb_ref, o_ref, s_ref, *, n_chunks, beta_lane0, g_lane0):
    h = pl.program_id(1)

    @pl.when(pl.program_id(2) == 0)
    def _():
        s_ref[...] = jnp.zeros_like(s_ref)

    c_n = GDN_CHUNK
    row = lax.broadcasted_iota(I32, (c_n, c_n), 0)
    col = lax.broadcasted_iota(I32, (c_n, c_n), 1)
    incl = row >= col
    strict = row > col
    tri = incl.astype(F32)
    eye = row == col
    ones = jnp.ones((c_n, c_n), F32)
    lane = lax.broadcasted_iota(I32, (c_n, LANES), 1)

    def chunk(ci, carry):
        r0 = pl.multiple_of(ci * c_n, c_n)
        q = q_ref[pl.ds(r0, c_n), :]
        k = k_ref[pl.ds(r0, c_n), :]
        v = v_ref[pl.ds(r0, c_n), :]
        gb = gb_ref[pl.ds(r0, c_n), :]
        beta = jnp.sum(jnp.where(lane == beta_lane0 + h, gb, 0.0), axis=1, keepdims=True)
        g = jnp.sum(jnp.where(lane == g_lane0 + h, gb, 0.0), axis=1, keepdims=True)
        gc = _dot_exact_lhs(tri, jnp.broadcast_to(g, (c_n, LANES)))
        gc_col = gc[:, :c_n]
        gc_row = _dot_exact_lhs(ones, jnp.where(eye, gc_col, 0.0))
        diff = gc_col - gc_row
        decay = jnp.where(incl, jnp.exp(jnp.where(incl, diff, 0.0)), 0.0)
        kb = k * beta
        vb = v * beta
        k16 = k.astype(BF16)
        a_kk = jnp.where(strict, _dot_nt(kb.astype(BF16), k16) * decay, 0.0)
        t_inv = _unit_lower_inverse(a_kk, row, col)
        e_gc = jnp.exp(gc)
        rhs = jnp.concatenate([vb, kb * e_gc], axis=1)
        sol = _dot_x3(t_inv, rhs)
        u_base, w_s = sol[:, :LANES], sol[:, LANES:]
        a_qk = _dot_nt(q.astype(BF16), k16) * decay
        gc_last = gc[c_n - 1:c_n, :]
        q_s = q * e_gc
        k_tail = k * jnp.exp(gc_last - gc)
        s_mat = s_ref[...]
        s16 = s_mat.astype(BF16)
        u = u_base - _dot(w_s.astype(BF16), s16)
        u16 = u.astype(BF16)
        o = _dot(q_s.astype(BF16), s16) + _dot(a_qk.astype(BF16), u16)
        s_ref[...] = s_mat * jnp.exp(gc_last) + _dot_tn(k_tail.astype(BF16), u16)
        o_ref[pl.ds(r0, c_n), :] = o
        return carry

    lax.fori_loop(0, n_chunks, chunk, 0)


def _gdn_scan(qkv, gb, d_model, beta_lane0, g_lane0):
    b, l, _ = qkv.shape
    n_heads = d_model // LANES
    tl = _pick(l, 1024, GDN_CHUNK)
    kern = functools.partial(_gdn_scan_kernel, n_chunks=tl // GDN_CHUNK, beta_lane0=beta_lane0, g_lane0=g_lane0)
    head_spec = lambda off: pl.BlockSpec((None, tl, LANES), lambda bi, h, i: (bi, i, off + h))
    return pl.pallas_call(
        kern,
        grid=(b, n_heads, l // tl),
        in_specs=[head_spec(0), head_spec(n_heads), head_spec(2 * n_heads),
                  pl.BlockSpec((None, tl, gb.shape[-1]), lambda bi, h, i: (bi, i, 0))],
        out_specs=pl.BlockSpec((None, tl, LANES), lambda bi, h, i: (bi, i, h)),
        out_shape=jax.ShapeDtypeStruct((b, l, d_model), F32),
        scratch_shapes=[pltpu.VMEM((LANES, LANES), F32)],
        compiler_params=_params("parallel", "parallel", "arbitrary"),
        name="gdn_scan",
    )(qkv, qkv, qkv, gb)


def _gdn_out_kernel(of_ref, ob_ref, gt_ref, w_ref, o_ref):
    o = of_ref[...] + ob_ref[...]
    ms = jnp.mean(o * o, axis=-1, keepdims=True)
    y = o * lax.rsqrt(ms + NORM_EPS) * w_ref[...]
    o_ref[...] = (y * _silu(gt_ref[...])).astype(o_ref.dtype)


def _gdn_out(o_f, o_b, proj, gate_block0, o_norm):
    b, l, d = o_f.shape
    tm = _pick(l, 1024)
    spec = pl.BlockSpec((None, tm, LANES), lambda bi, i, h: (bi, i, h))
    return pl.pallas_call(
        _gdn_out_kernel,
        grid=(b, l // tm, d // LANES),
        in_specs=[spec, spec,
                  pl.BlockSpec((None, tm, LANES), lambda bi, i, h: (bi, i, gate_block0 + h)),
                  pl.BlockSpec((1, LANES), lambda bi, i, h: (0, 0))],
        out_specs=spec,
        out_shape=jax.ShapeDtypeStruct((b, l, d), BF16),
        compiler_params=_params("parallel", "parallel", "parallel"),
        name="gdn_out",
    )(o_f, o_b, proj, o_norm.reshape(1, LANES))


def _gdn_mixer(x, g, sc, sh, gate, params):
    w_in, conv, a_log, dt_bias, o_norm, w_out = params
    d = x.shape[-1]
    n_heads = a_log.shape[1]
    assert d // n_heads == LANES and 4 * n_heads == LANES
    proj = _norm_matmul(x, g, sc, sh, w_in)
    qkv = _conv3(proj, conv, 3 * d, gdn_d=d)
    gb = _gdn_gates(proj, (4 * d) // LANES, a_log, dt_bias)
    o_f = _gdn_scan(qkv, gb, d, 0, 2 * n_heads)
    o_b = jnp.flip(_gdn_scan(jnp.flip(qkv, axis=1), jnp.flip(gb, axis=1), d, n_heads, 3 * n_heads), axis=1)
    y = _gdn_out(o_f, o_b, proj, (3 * d) // LANES, o_norm)
    return _resid_matmul(y, w_out, x, gate)


def _top_rows(s, k, payload=None):
    r_n = s.shape[0]
    row = lax.broadcasted_iota(I32, s.shape, 0)
    vals, idxs, pays = [], [], []
    for _ in range(k):
        m = jnp.max(s, axis=0, keepdims=True)
        idx = jnp.min(jnp.where(s == m, row, r_n), axis=0, keepdims=True)
        hit = row == idx
        vals.append(m)
        idxs.append(idx)
        if payload is not None:
            pays.append(jnp.max(jnp.where(hit, payload, -1), axis=0, keepdims=True))
        s = jnp.where(hit, -jnp.inf, s)
    out = (jnp.concatenate(vals, axis=0), jnp.concatenate(idxs, axis=0))
    if payload is not None:
        out += (jnp.concatenate(pays, axis=0),)
    return out


def _peer_route_kernel(q_ref, khi_ref, klo_ref, ids_ref, gates_ref, *, n_heads, n_keys, key_dim):
    k_top = PEER_TOPK
    for hd in range(n_heads):
        tops = []
        for p in range(2):
            c0 = (hd * 2 + p) * key_dim
            q_hi, q_lo = _split2(q_ref[:, c0:c0 + key_dim])
            s = _dot3(khi_ref[p], klo_ref[p], q_hi, q_lo, _dot_nt)
            tops.append(_top_rows(s, k_top))
        (s0, i0), (s1, i1) = tops
        cand = jnp.concatenate([s0[a:a + 1, :] + s1 for a in range(k_top)], axis=0)
        cid = jnp.concatenate([i0[a:a + 1, :] * n_keys + i1 for a in range(k_top)], axis=0)
        best, _, ids = _top_rows(cand, k_top, payload=cid)
        e = jnp.exp(best - best[0:1, :])
        gates = e / jnp.sum(e, axis=0, keepdims=True)
        ids_ref[hd * k_top:(hd + 1) * k_top, :] = ids
        gates_ref[hd * k_top:(hd + 1) * k_top, :] = gates


def _peer_route(q, keys):
    t_n, qd = q.shape
    _, n_keys, key_dim = keys.shape
    n_heads = qd // (2 * key_dim)
    tm = _pick(t_n, 256, LANES)
    k_hi, k_lo = _split2(keys)
    kern = functools.partial(_peer_route_kernel, n_heads=n_heads, n_keys=n_keys, key_dim=key_dim)
    rows = n_heads * PEER_TOPK
    k_spec = pl.BlockSpec((2, n_keys, key_dim), lambda i: (0, 0, 0))
    o_spec = pl.BlockSpec((rows, tm), lambda i: (0, i))
    return pl.pallas_call(
        kern,
        grid=(t_n // tm,),
        in_specs=[pl.BlockSpec((tm, qd), lambda i: (i, 0)), k_spec, k_spec],
        out_specs=[o_spec, o_spec],
        out_shape=[jax.ShapeDtypeStruct((rows, t_n), I32), jax.ShapeDtypeStruct((rows, t_n), F32)],
        compiler_params=_params("parallel"),
        name="peer_route",
    )(q, k_hi, k_lo)


def _peer_gather_kernel(ids_ref, gates_ref, x_ref, g_ref, sc_ref, sh_ref, gt_ref, u_hbm, v_hbm,
                        o_ref, h_ref, ubuf, vbuf, sem, *, tb, n_sel):
    h_ref[...] = _modulated_norm(x_ref[...], g_ref[...], sc_ref[...], sh_ref[...])

    def row_copy(tab, buf, t, j, slot, which):
        return pltpu.make_async_copy(tab.at[pl.ds(ids_ref[t, j], 1), :], buf.at[slot, pl.ds(j, 1), :],
                                     sem.at[which, slot])

    def issue(t, slot):
        def body(jo, carry):
            for ji in range(PEER_ISSUE_UNROLL):
                j = jo * PEER_ISSUE_UNROLL + ji
                row_copy(u_hbm, ubuf, t, j, slot, 0).start()
                row_copy(v_hbm, vbuf, t, j, slot, 1).start()
            return carry
        lax.fori_loop(0, n_sel // PEER_ISSUE_UNROLL, body, 0)

    def wait(t, slot):
        pltpu.make_async_copy(u_hbm.at[pl.ds(0, n_sel), :], ubuf.at[slot], sem.at[0, slot]).wait()
        pltpu.make_async_copy(v_hbm.at[pl.ds(0, n_sel), :], vbuf.at[slot], sem.at[1, slot]).wait()

    issue(0, 0)

    def token(t, carry):
        slot = lax.rem(t, 2)

        @pl.when(t + 1 < tb)
        def _():
            issue(t + 1, 1 - slot)

        wait(t, slot)
        h_row = h_ref[pl.ds(t, 1), :]
        act = jnp.sum(ubuf[slot] * h_row, axis=1, keepdims=True)
        wgt = gates_ref[t] * (0.5 * act * (1.0 + lax.erf(act * (2.0 ** -0.5))))
        y = jnp.sum(vbuf[slot] * wgt, axis=0, keepdims=True)
        o_ref[pl.ds(t, 1), :] = x_ref[pl.ds(t, 1), :] + gt_ref[...] * y
        return carry

    lax.fori_loop(0, tb, token, 0)


def _peer_gather(ids, gates, x, g, sc, sh, gate, u_tab, v_tab):
    b, l, d = x.shape
    t_n, n_sel = ids.shape
    tb = _pick(l, PEER_TOKENS_PER_STEP)
    per_b = l // tb
    xf = x.reshape(t_n, d)
    mod_spec = pl.BlockSpec((None, 1, d), lambda i: (i // per_b, 0, 0))
    kern = functools.partial(_peer_gather_kernel, tb=tb, n_sel=n_sel)
    out = pl.pallas_call(
        kern,
        grid=(t_n // tb,),
        in_specs=[pl.BlockSpec((tb, n_sel), lambda i: (i, 0), memory_space=pltpu.SMEM),
                  pl.BlockSpec((tb, n_sel, 1), lambda i: (i, 0, 0)),
                  pl.BlockSpec((tb, d), lambda i: (i, 0)),
                  pl.BlockSpec((1, d), lambda i: (0, 0)),
                  mod_spec, mod_spec, mod_spec,
                  pl.BlockSpec(memory_space=pl.ANY),
                  pl.BlockSpec(memory_space=pl.ANY)],
        out_specs=pl.BlockSpec((tb, d), lambda i: (i, 0)),
        out_shape=jax.ShapeDtypeStruct((t_n, d), F32),
        scratch_shapes=[pltpu.VMEM((tb, d), F32),
                        pltpu.VMEM((2, n_sel, d), F32),
                        pltpu.VMEM((2, n_sel, d), F32),
                        pltpu.SemaphoreType.DMA((2, 2))],
        compiler_params=_params("arbitrary"),
        name="peer_gather",
    )(ids, gates, xf, g.reshape(1, d), sc, sh, gate, u_tab, v_tab)
    return out.reshape(b, l, d)


def _peer(x, g, sc, sh, gate, params):
    w_q, keys, u_tab, v_tab = params
    b, l, d = x.shape
    q = _norm_matmul(x, g, sc, sh, w_q, three_pass=True)
    ids_t, gates_t = _peer_route(q.reshape(b * l, -1), keys)
    ids = ids_t.T
    gates = gates_t.T[:, :, None]
    return _peer_gather(ids, gates, x, g, sc, sh, gate, u_tab, v_tab)


def _final_norm_kernel(x_ref, g_ref, o_ref):
    x = x_ref[...]
    ms = jnp.mean(x * x, axis=-1, keepdims=True)
    o_ref[...] = x * lax.rsqrt(ms + NORM_EPS) * g_ref[...]


def _final_norm(x, g):
    b, l, d = x.shape
    tm = _pick(l, 256)
    return pl.pallas_call(
        _final_norm_kernel,
        grid=(b, l // tm),
        in_specs=[pl.BlockSpec((None, tm, d), lambda bi, i: (bi, i, 0)),
                  pl.BlockSpec((1, d), lambda bi, i: (0, 0))],
        out_specs=pl.BlockSpec((None, tm, d), lambda bi, i: (bi, i, 0)),
        out_shape=jax.ShapeDtypeStruct((b, l, d), F32),
        compiler_params=_params("parallel", "parallel"),
        name="final_norm",
    )(x, g.reshape(1, d))


_MIXERS = (_hyena_mixer, _pool_mixer, _gdn_mixer)


def _trunk(x, mods, layers, final_norm):
    d = x.shape[-1]
    for li, (norm_tok, norm_ch, mixer_params, peer_params) in enumerate(layers):
        mod = mods[li].reshape(x.shape[0], 1, N_MOD, d)
        sh_t, sc_t, g_t, sh_c, sc_c, g_c = (mod[:, :, m] for m in range(N_MOD))
        x = _MIXERS[li % len(_MIXERS)](x, norm_tok, sc_t, sh_t, g_t, mixer_params)
        x = _peer(x, norm_ch, sc_c, sh_c, g_c, peer_params)
    return _final_norm(x, final_norm)


def kernel(x_prompt, x_sample, c_prompt, c_sample, l0_ada_w, l0_ada_b, l0_norm_tok, l0_norm_ch, l0_hy_w_in, l0_hy_conv, l0_hy_ffn_w1, l0_hy_ffn_b1, l0_hy_ffn_w2, l0_hy_ffn_b2, l0_hy_ffn_w3, l0_hy_ffn_b3, l0_hy_sin_freq, l0_hy_log_decay, l0_hy_bias, l0_hy_w_out, l0_peer_w_q, l0_peer_keys, l0_peer_u, l0_peer_v, l1_ada_w, l1_ada_b, l1_norm_tok, l1_norm_ch, l1_pool_w_in, l1_pool_w_grp, l1_pool_scale, l1_pool_w_out, l1_peer_w_q, l1_peer_keys, l1_peer_u, l1_peer_v, l2_ada_w, l2_ada_b, l2_norm_tok, l2_norm_ch, l2_gdn_w_in, l2_gdn_conv, l2_gdn_A_log, l2_gdn_dt_bias, l2_gdn_o_norm, l2_gdn_w_out, l2_peer_w_q, l2_peer_keys, l2_peer_u, l2_peer_v, l3_ada_w, l3_ada_b, l3_norm_tok, l3_norm_ch, l3_hy_w_in, l3_hy_conv, l3_hy_ffn_w1, l3_hy_ffn_b1, l3_hy_ffn_w2, l3_hy_ffn_b2, l3_hy_ffn_w3, l3_hy_ffn_b3, l3_hy_sin_freq, l3_hy_log_decay, l3_hy_bias, l3_hy_w_out, l3_peer_w_q, l3_peer_keys, l3_peer_u, l3_peer_v, final_norm):
    layers = (
        (l0_norm_tok, l0_norm_ch,
         (l0_hy_w_in, l0_hy_conv, l0_hy_ffn_w1, l0_hy_ffn_b1, l0_hy_ffn_w2, l0_hy_ffn_b2,
          l0_hy_ffn_w3, l0_hy_ffn_b3, l0_hy_sin_freq, l0_hy_log_decay, l0_hy_bias, l0_hy_w_out),
         (l0_peer_w_q, l0_peer_keys, l0_peer_u, l0_peer_v)),
        (l1_norm_tok, l1_norm_ch,
         (l1_pool_w_in, l1_pool_w_grp, l1_pool_scale, l1_pool_w_out),
         (l1_peer_w_q, l1_peer_keys, l1_peer_u, l1_peer_v)),
        (l2_norm_tok, l2_norm_ch,
         (l2_gdn_w_in, l2_gdn_conv, l2_gdn_A_log, l2_gdn_dt_bias, l2_gdn_o_norm, l2_gdn_w_out),
         (l2_peer_w_q, l2_peer_keys, l2_peer_u, l2_peer_v)),
        (l3_norm_tok, l3_norm_ch,
         (l3_hy_w_in, l3_hy_conv, l3_hy_ffn_w1, l3_hy_ffn_b1, l3_hy_ffn_w2, l3_hy_ffn_b2,
          l3_hy_ffn_w3, l3_hy_ffn_b3, l3_hy_sin_freq, l3_hy_log_decay, l3_hy_bias, l3_hy_w_out),
         (l3_peer_w_q, l3_peer_keys, l3_peer_u, l3_peer_v)),
    )
    ada = ((l0_ada_w, l0_ada_b), (l1_ada_w, l1_ada_b), (l2_ada_w, l2_ada_b), (l3_ada_w, l3_ada_b))
    n_p = c_prompt.shape[0]
    n_s = c_sample.shape[0]
    pad = (-(n_p + n_s)) % SUBLANES
    c_rows = jnp.concatenate([c_prompt, c_sample, jnp.zeros((pad, c_prompt.shape[1]), F32)], axis=0)
    mods = [_ada(c_rows, w, b) for w, b in ada]
    y_prompt = _trunk(x_prompt, [m[:n_p] for m in mods], layers, final_norm)
    y_sample = _trunk(x_sample, [m[n_p:n_p + n_s] for m in mods], layers, final_norm)
    return (y_prompt, y_sample)
```

```python
import functools
import math

import jax
import jax.numpy as jnp
from jax import lax
from jax.experimental import pallas as pl
from jax.experimental.pallas import tpu as pltpu

F32 = jnp.float32
BF16 = jnp.bfloat16
I32 = jnp.int32
HIGHEST = lax.Precision.HIGHEST

NORM_EPS = 1e-6
N_MOD = 6
LANES = 128
SUBLANES = 8
VMEM_LIMIT_BYTES = 56 * 1024 * 1024
FFT_N2 = 128
POOL_WINDOWS = (2, 4, 8, 16)
POOL_HALO = 8
GDN_CHUNK = 64
PEER_TOPK = 16
PEER_TOKENS_PER_STEP = 128
PEER_SLOTS = 8


def _pick(n, pref, mult=SUBLANES):
    t = (min(pref, n) // mult) * mult
    while t >= mult:
        if n % t == 0:
            return t
        t -= mult
    return n


def _params(*sem):
    return pltpu.CompilerParams(dimension_semantics=sem, vmem_limit_bytes=VMEM_LIMIT_BYTES)


def _split2(x):
    hi = x.astype(BF16)
    lo = (x - hi.astype(F32)).astype(BF16)
    return hi, lo


def _split3(x):
    hi = x.astype(BF16)
    r = x - hi.astype(F32)
    mid = r.astype(BF16)
    lo = (r - mid.astype(F32)).astype(BF16)
    return hi, mid, lo


def _dot(a, b):
    return jnp.dot(a, b, preferred_element_type=F32)


def _dot_nt(a, b):
    return lax.dot_general(a, b, (((1,), (1,)), ((), ())), preferred_element_type=F32)


def _dot_tn(a, b):
    return lax.dot_general(a, b, (((0,), (0,)), ((), ())), preferred_element_type=F32)


def _dot3(a_hi, a_lo, b_hi, b_lo, dot=_dot):
    return dot(a_hi, b_hi) + dot(a_lo, b_hi) + dot(a_hi, b_lo)


def _dot_x3(a, b, dot=_dot):
    a_hi, a_lo = _split2(a)
    b_hi, b_lo = _split2(b)
    return _dot3(a_hi, a_lo, b_hi, b_lo, dot)


def _dot_exact_lhs(a_exact, b):
    a = a_exact.astype(BF16)
    b_hi, b_mid, b_lo = _split3(b)
    return _dot(a, b_hi) + _dot(a, b_mid) + _dot(a, b_lo)


def _silu(x):
    return x * jax.nn.sigmoid(x)


def _ada_kernel(c_ref, w_ref, b_ref, o_ref):
    a = _silu(c_ref[...])
    o_ref[...] = jnp.dot(a, w_ref[...], precision=HIGHEST, preferred_element_type=F32) + b_ref[...]


def _ada(c_rows, w, b):
    rows, d = c_rows.shape
    n = w.shape[1]
    tn = _pick(n, 512, LANES)
    return pl.pallas_call(
        _ada_kernel,
        grid=(n // tn,),
        in_specs=[pl.BlockSpec((rows, d), lambda j: (0, 0)),
                  pl.BlockSpec((d, tn), lambda j: (0, j)),
                  pl.BlockSpec((1, tn), lambda j: (0, j))],
        out_specs=pl.BlockSpec((rows, tn), lambda j: (0, j)),
        out_shape=jax.ShapeDtypeStruct((rows, n), F32),
        compiler_params=_params("parallel"),
        name="ada_mod",
    )(c_rows, w, b.reshape(1, n))


def _modulated_norm(x, g, sc, sh):
    ms = jnp.mean(x * x, axis=-1, keepdims=True)
    return (x * lax.rsqrt(ms + NORM_EPS) * g) * (1.0 + sc) + sh


def _nm_kernel(x_ref, g_ref, sc_ref, sh_ref, w_ref, o_ref, h_ref):
    @pl.when(pl.program_id(2) == 0)
    def _():
        h_ref[...] = _modulated_norm(x_ref[...], g_ref[...], sc_ref[...], sh_ref[...]).astype(BF16)

    o_ref[...] = _dot(h_ref[...], w_ref[...]).astype(o_ref.dtype)


def _nm3_kernel(x_ref, g_ref, sc_ref, sh_ref, whi_ref, wlo_ref, o_ref, hhi_ref, hlo_ref):
    @pl.when(pl.program_id(2) == 0)
    def _():
        hi, lo = _split2(_modulated_norm(x_ref[...], g_ref[...], sc_ref[...], sh_ref[...]))
        hhi_ref[...] = hi
        hlo_ref[...] = lo

    o_ref[...] = _dot3(hhi_ref[...], hlo_ref[...], whi_ref[...], wlo_ref[...]).astype(o_ref.dtype)


def _norm_matmul(x, g, sc, sh, w, three_pass=False, out_dtype=F32):
    b, l, d = x.shape
    n = w.shape[1]
    tm = _pick(l, 256)
    tn = _pick(n, 512, LANES)
    x_spec = pl.BlockSpec((None, tm, d), lambda bi, i, j: (bi, i, 0))
    g_spec = pl.BlockSpec((1, d), lambda bi, i, j: (0, 0))
    m_spec = pl.BlockSpec((None, 1, d), lambda bi, i, j: (bi, 0, 0))
    w_spec = pl.BlockSpec((d, tn), lambda bi, i, j: (0, j))
    o_spec = pl.BlockSpec((None, tm, tn), lambda bi, i, j: (bi, i, j))
    if three_pass:
        w_hi, w_lo = _split2(w)
        kern, w_args, w_specs = _nm3_kernel, (w_hi, w_lo), [w_spec, w_spec]
        scratch = [pltpu.VMEM((tm, d), BF16), pltpu.VMEM((tm, d), BF16)]
    else:
        kern, w_args, w_specs = _nm_kernel, (w.astype(BF16),), [w_spec]
        scratch = [pltpu.VMEM((tm, d), BF16)]
    return pl.pallas_call(
        kern,
        grid=(b, l // tm, n // tn),
        in_specs=[x_spec, g_spec, m_spec, m_spec] + w_specs,
        out_specs=o_spec,
        out_shape=jax.ShapeDtypeStruct((b, l, n), out_dtype),
        scratch_shapes=scratch,
        compiler_params=_params("parallel", "parallel", "arbitrary"),
        name="norm_matmul3" if three_pass else "norm_matmul",
    )(x, g.reshape(1, d), sc, sh, *w_args)


def _resid_mm_kernel(z_ref, w_ref, x_ref, gt_ref, o_ref):
    o_ref[...] = x_ref[...] + gt_ref[...] * _dot(z_ref[...].astype(BF16), w_ref[...])


def _resid_matmul(z, w, x, gate):
    b, l, k = z.shape
    n = w.shape[1]
    tm = _pick(l, 512)
    tn = _pick(n, 512, LANES)
    return pl.pallas_call(
        _resid_mm_kernel,
        grid=(b, l // tm, n // tn),
        in_specs=[pl.BlockSpec((None, tm, k), lambda bi, i, j: (bi, i, 0)),
                  pl.BlockSpec((k, tn), lambda bi, i, j: (0, j)),
                  pl.BlockSpec((None, tm, tn), lambda bi, i, j: (bi, i, j)),
                  pl.BlockSpec((None, 1, tn), lambda bi, i, j: (bi, 0, j))],
        out_specs=pl.BlockSpec((None, tm, tn), lambda bi, i, j: (bi, i, j)),
        out_shape=jax.ShapeDtypeStruct((b, l, n), F32),
        compiler_params=_params("parallel", "parallel", "parallel"),
        name="resid_matmul",
    )(z, w.astype(BF16), x, gate)


def _conv3_kernel(x_ref, p_ref, n_ref, w_ref, o_ref, *, tm, gdn_d, head_scale):
    i = pl.program_id(1)
    last = pl.num_programs(1) - 1
    x = x_ref[...]
    rows = lax.broadcasted_iota(I32, (tm, 1), 0)
    prev_row = jnp.where(i > 0, p_ref[SUBLANES - 1:SUBLANES, :], 0.0)
    next_row = jnp.where(i < last, n_ref[0:1, :], 0.0)
    x_m = jnp.where(rows == 0, prev_row, pltpu.roll(x, 1, axis=0))
    x_p = jnp.where(rows == tm - 1, next_row, pltpu.roll(x, tm - 1, axis=0))
    w = w_ref[...]
    y = x_m * w[0:1, :] + x * w[1:2, :] + x_p * w[2:3, :]
    if gdn_d:
        y = _silu(y)
        sec = (pl.program_id(2) * LANES) // gdn_d
        nrm = y * lax.rsqrt(jnp.sum(y * y, axis=-1, keepdims=True) + NORM_EPS)
        nrm = nrm * jnp.where(sec == 0, head_scale, 1.0)
        y = jnp.where(sec < 2, nrm, y)
    o_ref[...] = y


def _conv3(x, w, n_cols, gdn_d=0):
    b, l, _ = x.shape
    tm = _pick(l, 512)
    tc = LANES if gdn_d else _pick(n_cols, 1024, LANES)
    r8 = tm // SUBLANES
    kern = functools.partial(_conv3_kernel, tm=tm, gdn_d=gdn_d, head_scale=float(LANES) ** -0.5)
    return pl.pallas_call(
        kern,
        grid=(b, l // tm, n_cols // tc),
        in_specs=[pl.BlockSpec((None, tm, tc), lambda bi, i, c: (bi, i, c)),
                  pl.BlockSpec((None, SUBLANES, tc), lambda bi, i, c: (bi, jnp.maximum(i * r8 - 1, 0), c)),
                  pl.BlockSpec((None, SUBLANES, tc),
                               lambda bi, i, c: (bi, jnp.minimum((i + 1) * r8, l // SUBLANES - 1), c)),
                  pl.BlockSpec((3, tc), lambda bi, i, c: (0, c))],
        out_specs=pl.BlockSpec((None, tm, tc), lambda bi, i, c: (bi, i, c)),
        out_shape=jax.ShapeDtypeStruct((b, l, n_cols), F32),
        compiler_params=_params("parallel", "parallel", "parallel"),
        name="conv3_gdn" if gdn_d else "conv3",
    )(x, x, x, w)


def _hyfilt_kernel(band_ref, w1_ref, b1_ref, w2_ref, b2_ref, sf_ref, w3_ref, b3_ref, ld_ref,
                   o_ref, ss_ref, *, tl, seq, n_band, d_model):
    j = pl.program_id(0)
    i = pl.program_id(1)
    pos = (i * tl + lax.broadcasted_iota(I32, (tl, 1), 0)).astype(F32)
    t = pos / float(seq - 1)
    omega = (2.0 * math.pi) * pos / float(seq)
    ang = omega * band_ref[...]
    lane = lax.broadcasted_iota(I32, (tl, LANES), 1)
    feats = jnp.where(lane == 0, t,
                      jnp.where(lane <= n_band, jnp.cos(ang),
                                jnp.where(lane <= 2 * n_band, -jnp.sin(ang), 0.0)))
    sf = sf_ref[...]
    h = jnp.sin(sf[0:1, :] * (jnp.dot(feats, w1_ref[...], precision=HIGHEST) + b1_ref[...]))
    h = jnp.sin(sf[1:2, :] * (jnp.dot(h, w2_ref[...], precision=HIGHEST) + b2_ref[...]))
    h = jnp.dot(h, w3_ref[...], precision=HIGHEST) + b3_ref[...]
    out = h * jnp.exp(-t * jnp.exp(ld_ref[...]))
    o_ref[...] = out

    tn = out.shape[1]
    is_bwd = ((j * tn) // d_model) % 2 == 1
    sq = jnp.where(jnp.logical_and(is_bwd, pos == 0.0), 0.0, out * out)

    @pl.when(i == 0)
    def _():
        ss_ref[...] = jnp.zeros_like(ss_ref)

    ss_ref[...] += jnp.sum(sq, axis=0, keepdims=True)


def _hyena_filters(seq, w1, b1, w2, b2, w3, b3, sin_freq, log_decay):
    emb, hid = w1.shape
    n_band = (emb - 1) // 2
    n_out = w3.shape[1]
    d_model = log_decay.shape[-1]
    bands = jnp.linspace(1e-4, n_band - 1, n_band, dtype=F32)
    band_row = jnp.zeros((1, LANES), F32).at[0, 1:1 + n_band].set(bands).at[0, 1 + n_band:1 + 2 * n_band].set(bands)
    w1p = jnp.zeros((LANES, hid), F32).at[:emb].set(w1)
    tl = _pick(seq, 512)
    tn = _pick(d_model, 1024, LANES)
    kern = functools.partial(_hyfilt_kernel, tl=tl, seq=seq, n_band=n_band, d_model=d_model)
    full = lambda shape: pl.BlockSpec(shape, lambda j, i: (0,) * len(shape))
    return pl.pallas_call(
        kern,
        grid=(n_out // tn, seq // tl),
        in_specs=[full((1, LANES)), full((LANES, hid)), full((1, hid)), full((hid, hid)), full((1, hid)),
                  full((2, hid)),
                  pl.BlockSpec((hid, tn), lambda j, i: (0, j)),
                  pl.BlockSpec((1, tn), lambda j, i: (0, j)),
                  pl.BlockSpec((1, tn), lambda j, i: (0, j))],
        out_specs=[pl.BlockSpec((tl, tn), lambda j, i: (i, j)),
                   pl.BlockSpec((1, tn), lambda j, i: (0, j))],
        out_shape=[jax.ShapeDtypeStruct((seq, n_out), F32), jax.ShapeDtypeStruct((1, n_out), F32)],
        compiler_params=_params("parallel", "arbitrary"),
        name="hyena_filters",
    )(band_row, w1p, b1.reshape(1, hid), w2, b2.reshape(1, hid), sin_freq, w3, b3.reshape(1, n_out),
      log_decay.reshape(1, n_out))


def _lmm_kernel(mh_ref, ml_ref, x_ref, o_ref, *, gb):
    for g in range(gb):
        x_hi, x_lo = _split2(x_ref[g])
        o_ref[g] = _dot3(mh_ref[g], ml_ref[g], x_hi, x_lo)


def _lmm_epi_kernel(mh_ref, ml_ref, x_ref, z_ref, gt_ref, bias_ref, o_ref, *, gb):
    for g in range(gb):
        x_hi, x_lo = _split2(x_ref[g])
        conv = _dot3(mh_ref[g], ml_ref[g], x_hi, x_lo)
        z = z_ref[g]
        o_ref[g] = gt_ref[g] * (conv + z * bias_ref[...])


def _fft_stage1(mats, x, epilogue=None):
    m_hi, m_lo = mats
    g_n, r_out, r_in = m_hi.shape
    d = x.shape[-1]
    gb = _pick(g_n, 4, 1)
    td = _pick(d, 1024, LANES)
    m_spec = pl.BlockSpec((gb, r_out, r_in), lambda g, j: (g, 0, 0))
    x_spec = pl.BlockSpec((gb, r_in, td), lambda g, j: (g, 0, j))
    o_spec = pl.BlockSpec((gb, r_out, td), lambda g, j: (g, 0, j))
    if epilogue is None:
        kern, extra, extra_specs = functools.partial(_lmm_kernel, gb=gb), (), []
    else:
        kern = functools.partial(_lmm_epi_kernel, gb=gb)
        extra = epilogue
        extra_specs = [o_spec, o_spec, pl.BlockSpec((1, td), lambda g, j: (0, j))]
    return pl.pallas_call(
        kern,
        grid=(g_n // gb, d // td),
        in_specs=[m_spec, m_spec, x_spec] + extra_specs,
        out_specs=o_spec,
        out_shape=jax.ShapeDtypeStruct((g_n, r_out, d), F32),
        compiler_params=_params("parallel", "parallel"),
        name="fft_stage1" if epilogue is None else "fft_stage1_inv",
    )(m_hi, m_lo, x, *extra)


def _s2f_kernel(mh_ref, ml_ref, x_ref, o_ref, *, gb):
    for g in range(gb):
        x_hi, x_lo = _split2(x_ref[g])
        o_ref[g] = _dot3(mh_ref[...], ml_ref[...], x_hi, x_lo)


def _s2c_kernel(mh_ref, ml_ref, ih_ref, il_ref, x_ref, f_ref, ss_ref, o_ref, *, gb, half):
    scale = lax.rsqrt(ss_ref[...] + NORM_EPS)
    for g in range(gb):
        x_hi, x_lo = _split2(x_ref[g])
        spec = _dot3(mh_ref[...], ml_ref[...], x_hi, x_lo)
        f = f_ref[g]
        s_re, s_im = spec[:half], spec[half:]
        f_re, f_im = f[:half], f[half:]
        prod = jnp.concatenate([s_re * f_re - s_im * f_im, s_re * f_im + s_im * f_re], axis=0) * scale
        p_hi, p_lo = _split2(prod)
        o_ref[g] = _dot3(ih_ref[...], il_ref[...], p_hi, p_lo)


def _fft_stage2_fwd(mats, x):
    m_hi, m_lo = mats
    r = m_hi.shape[0]
    g_n, _, d = x.shape
    gb = _pick(g_n, 4, 1)
    td = _pick(d, 1024, LANES)
    m_spec = pl.BlockSpec((r, r), lambda g, j: (0, 0))
    x_spec = pl.BlockSpec((gb, r, td), lambda g, j: (g, 0, j))
    return pl.pallas_call(
        functools.partial(_s2f_kernel, gb=gb),
        grid=(g_n // gb, d // td),
        in_specs=[m_spec, m_spec, x_spec],
        out_specs=x_spec,
        out_shape=jax.ShapeDtypeStruct(x.shape, F32),
        compiler_params=_params("parallel", "parallel"),
        name="fft_stage2_fwd",
    )(m_hi, m_lo, x)


def _fft_stage2_conv(mats, imats, x, filt_spec, sumsq):
    m_hi, m_lo = mats
    i_hi, i_lo = imats
    r = m_hi.shape[0]
    g_n, _, d = x.shape
    gb = _pick(g_n, 4, 1)
    td = _pick(d, 1024, LANES)
    m_spec = pl.BlockSpec((r, r), lambda g, j: (0, 0))
    x_spec = pl.BlockSpec((gb, r, td), lambda g, j: (g, 0, j))
    return pl.pallas_call(
        functools.partial(_s2c_kernel, gb=gb, half=r // 2),
        grid=(g_n // gb, d // td),
        in_specs=[m_spec, m_spec, m_spec, m_spec, x_spec, x_spec, pl.BlockSpec((1, td), lambda g, j: (0, j))],
        out_specs=x_spec,
        out_shape=jax.ShapeDtypeStruct(x.shape, F32),
        compiler_params=_params("parallel", "parallel"),
        name="fft_stage2_conv",
    )(m_hi, m_lo, i_hi, i_lo, x, filt_spec, sumsq)


def _dft_tables(seq):
    n = 2 * seq
    n2 = FFT_N2
    n1 = n // n2
    n1h = n1 // 2
    k1 = jnp.arange(n1, dtype=I32)
    n2i = jnp.arange(n2, dtype=I32)
    n1i = jnp.arange(n1, dtype=I32)
    pos = n1i[None, :] * n2 + n2i[:, None]
    prod = (k1[None, :, None] * pos[:, None, :]) % n
    ang = prod.astype(F32) * (2.0 * math.pi / n)
    c, s = jnp.cos(ang), jnp.sin(ang)
    ch, sh = c[:, :, :n1h], s[:, :, :n1h]
    fwd_data = jnp.concatenate([jnp.concatenate([ch, sh], axis=2),
                                jnp.concatenate([-sh, ch], axis=2)], axis=1)
    fwd_real = jnp.concatenate([c, -s], axis=1)
    ct, st = jnp.swapaxes(ch, 1, 2), jnp.swapaxes(sh, 1, 2)
    inv_data = jnp.concatenate([jnp.concatenate([ct, -st], axis=2),
                                jnp.concatenate([st, ct], axis=2)], axis=1) * (1.0 / n)
    a2 = ((n2i[:, None] * n2i[None, :]) % n2).astype(F32) * (2.0 * math.pi / n2)
    c2, s2 = jnp.cos(a2), jnp.sin(a2)
    m2 = jnp.concatenate([jnp.concatenate([c2, s2], axis=1), jnp.concatenate([-s2, c2], axis=1)], axis=0)
    m2i = jnp.concatenate([jnp.concatenate([c2, -s2], axis=1), jnp.concatenate([s2, c2], axis=1)], axis=0)
    return dict(n1=n1, n1h=n1h, fwd_data=_split2(fwd_data), fwd_real=_split2(fwd_real),
                inv_data=_split2(inv_data), m2=_split2(m2), m2i=_split2(m2i))


def _to_stage2_layout(a, n1):
    n2, r, d = a.shape
    c = r // n1
    return a.reshape(n2, c, n1, d).transpose(2, 1, 0, 3).reshape(n1, c * n2, d)


def _to_stage1_layout(a, n2):
    n1, r, d = a.shape
    c = r // n2
    return a.reshape(n1, c, n2, d).transpose(2, 1, 0, 3).reshape(n2, c * n1, d)


def _hyena_core(u, filt, sumsq, bias):
    b, seq, d3 = u.shape
    d = d3 // 3
    tab = _dft_tables(seq)
    n1, n1h, n2 = tab["n1"], tab["n1h"], FFT_N2
    ut = u.reshape(b, n1h, n2, 3, d).transpose(3, 2, 0, 1, 4).reshape(3, n2, b * n1h, d)
    z_t = ut[0]
    filt = filt.reshape(seq, 2, 2, d)
    for o in range(2):
        h_f, h_b = filt[:, o, 0], filt[:, o, 1]
        two_sided = jnp.concatenate([h_f, jnp.zeros((1, d), F32), h_b[:0:-1]], axis=0)
        f_t = two_sided.reshape(n1, n2, d).transpose(1, 0, 2)
        f_spec = _fft_stage2_fwd(tab["m2"], _to_stage2_layout(_fft_stage1(tab["fwd_real"], f_t), n1))
        ss = sumsq.reshape(2, 2, d)[o]
        ss = (ss[0] + ss[1]).reshape(1, d)
        a = _to_stage2_layout(_fft_stage1(tab["fwd_data"], z_t), n1)
        c = _to_stage1_layout(_fft_stage2_conv(tab["m2"], tab["m2i"], a, f_spec, ss), n2)
        z_t = _fft_stage1(tab["inv_data"], c, epilogue=(z_t, ut[1 + o], bias[o].reshape(1, d)))
    return z_t.reshape(n2, b, n1h, d).transpose(1, 2, 0, 3).reshape(b, seq, d)


def _hyena_mixer(x, g, sc, sh, gate, params):
    w_in, conv, w1, b1, w2, b2, w3, b3, sin_freq, log_decay, bias, w_out = params
    seq = x.shape[1]
    y = _norm_matmul(x, g, sc, sh, w_in)
    u = _conv3(y, conv, y.shape[-1])
    filt, sumsq = _hyena_filters(seq, w1, b1, w2, b2, w3, b3, sin_freq, log_decay)
    z = _hyena_core(u, filt, sumsq, bias)
    return _resid_matmul(z, w_out, x, gate)


def _pool_kernel(x_ref, p_ref, n_ref, o_ref, *, tm, seq, group):
    i = pl.program_id(1)
    last = pl.num_programs(1) - 1
    x = x_ref[...]
    prev = jnp.where(i > 0, p_ref[...], 0.0)
    nxt = jnp.where(i < last, n_ref[...], 0.0)
    ext = jnp.concatenate([prev, x, nxt], axis=0)
    gi = (pl.program_id(2) * x.shape[1]) // group
    half = jnp.left_shift(1, gi)
    r = lax.broadcasted_iota(I32, (tm, tm + 2 * POOL_HALO), 0)
    c = lax.broadcasted_iota(I32, (tm, tm + 2 * POOL_HALO), 1)
    band = jnp.logical_and(c >= r + POOL_HALO - half, c < r + POOL_HALO + half).astype(F32)
    win = _dot_exact_lhs(band, ext)
    t = i * tm + lax.broadcasted_iota(I32, (tm, 1), 0)
    cnt = (jnp.minimum(t + half, seq) - jnp.maximum(t - half, 0)).astype(F32)
    o_ref[...] = win / cnt - x


def _pool_windows(uf):
    b, l, d = uf.shape
    group = d // len(POOL_WINDOWS)
    tm = _pick(l, 256)
    tc = _pick(group, 512, LANES)
    r8 = tm // SUBLANES
    return pl.pallas_call(
        functools.partial(_pool_kernel, tm=tm, seq=l, group=group),
        grid=(b, l // tm, d // tc),
        in_specs=[pl.BlockSpec((None, tm, tc), lambda bi, i, c: (bi, i, c)),
                  pl.BlockSpec((None, SUBLANES, tc), lambda bi, i, c: (bi, jnp.maximum(i * r8 - 1, 0), c)),
                  pl.BlockSpec((None, SUBLANES, tc),
                               lambda bi, i, c: (bi, jnp.minimum((i + 1) * r8, l // SUBLANES - 1), c))],
        out_specs=pl.BlockSpec((None, tm, tc), lambda bi, i, c: (bi, i, c)),
        out_shape=jax.ShapeDtypeStruct((b, l, d), F32),
        compiler_params=_params("parallel", "parallel", "parallel"),
        name="pool_windows",
    )(uf, uf, uf)


def _group_mm_kernel(p_ref, w_ref, s_ref, o_ref):
    o_ref[...] = (_dot(p_ref[...].astype(BF16), w_ref[...]) * s_ref[...]).astype(o_ref.dtype)


def _group_matmul(p, w_grp, scale):
    b, l, d = p.shape
    n_g, gd, _ = w_grp.shape
    tm = _pick(l, 512)
    tn = _pick(gd, 512, LANES)
    per = gd // tn
    return pl.pallas_call(
        _group_mm_kernel,
        grid=(b, l // tm, n_g, per),
        in_specs=[pl.BlockSpec((None, tm, gd), lambda bi, i, g, j: (bi, i, g)),
                  pl.BlockSpec((None, gd, tn), lambda bi, i, g, j: (g, 0, j)),
                  pl.BlockSpec((1, tn), lambda bi, i, g, j: (0, g * per + j))],
        out_specs=pl.BlockSpec((None, tm, tn), lambda bi, i, g, j: (bi, i, g * per + j)),
        out_shape=jax.ShapeDtypeStruct((b, l, d), BF16),
        compiler_params=_params("parallel", "parallel", "parallel", "parallel"),
        name="pool_group_matmul",
    )(p, w_grp.astype(BF16), scale.reshape(1, d))


def _pool_mixer(x, g, sc, sh, gate, params):
    w_in, w_grp, scale, w_out = params
    uf = _norm_matmul(x, g, sc, sh, w_in)
    y = _group_matmul(_pool_windows(uf), w_grp, scale)
    return _resid_matmul(y, w_out, x, gate)


def _gdn_gates_kernel(x_ref, alog_ref, dtb_ref, o_ref, *, n_heads):
    x = x_ref[...]
    lane = lax.broadcasted_iota(I32, x.shape, 1)
    decay = -jnp.exp(alog_ref[...]) * jax.nn.softplus(x + dtb_ref[...])
    o_ref[...] = jnp.where(lane < 2 * n_heads, jax.nn.sigmoid(x), decay)


def _gdn_gates(proj, col_block, a_log, dt_bias):
    b, l, _ = proj.shape
    n_heads = a_log.shape[1]
    pad = jnp.zeros((1, 2 * n_heads), F32)
    alog_row = jnp.concatenate([pad, a_log.reshape(1, 2 * n_heads)], axis=1)
    dtb_row = jnp.concatenate([pad, dt_bias.reshape(1, 2 * n_heads)], axis=1)
    tm = _pick(l, 1024)
    w = 4 * n_heads
    return pl.pallas_call(
        functools.partial(_gdn_gates_kernel, n_heads=n_heads),
        grid=(b, l // tm),
        in_specs=[pl.BlockSpec((None, tm, w), lambda bi, i: (bi, i, col_block)),
                  pl.BlockSpec((1, w), lambda bi, i: (0, 0)),
                  pl.BlockSpec((1, w), lambda bi, i: (0, 0))],
        out_specs=pl.BlockSpec((None, tm, w), lambda bi, i: (bi, i, 0)),
        out_shape=jax.ShapeDtypeStruct((b, l, w), F32),
        compiler_params=_params("parallel", "parallel"),
        name="gdn_gates",
    )(proj, alog_row, dtb_row)


def _unit_triangular_inverse(a, row, col):
    n = a[0].shape[0]
    eye = (row == col).astype(F32)
    blk = lambda s: (row // s) == (col // s)
    a8 = [jnp.where(blk(8), x, 0.0) for x in a]
    a8_2 = _each(_dot_x3, a8, a8)
    a8_4 = _each(_dot_x3, a8_2, a8_2)
    t = _each(_dot_x3, [eye - x for x in a8], [eye + x for x in a8_2])
    t = _each(_dot_x3, t, [eye + x for x in a8_4])
    s = 8
    while s < n:
        mask = jnp.logical_and(blk(2 * s), jnp.logical_not(blk(s)))
        off = [jnp.where(mask, x, 0.0) for x in a]
        corr = _each(_dot_x3, _each(_dot_x3, t, off), t)
        t = [x - y for x, y in zip(t, corr)]
        s *= 2
    return t


def _each(fn, *lists):
    return [fn(*xs) for xs in zip(*lists)]


def _gdn_chunks(chains, head, row, col, lane):
    c_n = GDN_CHUNK
    q, k, v, gb, s_mat, rev, beta0, g0 = (list(z) for z in zip(*chains))
    incl = [(row <= col) if r else (row >= col) for r in rev]
    strict = [(row < col) if r else (row > col) for r in rev]
    edge = [0 if r else c_n - 1 for r in rev]
    eye = row == col
    ones = jnp.ones((c_n, c_n), F32)
    beta = [jnp.sum(jnp.where(lane == b0 + head, x, 0.0), axis=1, keepdims=True) for x, b0 in zip(gb, beta0)]
    g = [jnp.sum(jnp.where(lane == b0 + head, x, 0.0), axis=1, keepdims=True) for x, b0 in zip(gb, g0)]
    gc = [_dot_exact_lhs(m.astype(F32), jnp.broadcast_to(x, (c_n, LANES))) for m, x in zip(incl, g)]
    gc_col = [x[:, :c_n] for x in gc]
    gc_row = [_dot_exact_lhs(ones, jnp.where(eye, x, 0.0)) for x in gc_col]
    decay = [jnp.where(m, jnp.exp(jnp.where(m, c - r, 0.0)), 0.0) for m, c, r in zip(incl, gc_col, gc_row)]
    kb = [x * b for x, b in zip(k, beta)]
    vb = [x * b for x, b in zip(v, beta)]
    k16 = [x.astype(BF16) for x in k]
    kk = _each(_dot_nt, [x.astype(BF16) for x in kb], k16)
    a_kk = [jnp.where(m, x * d, 0.0) for m, x, d in zip(strict, kk, decay)]
    t_inv = _unit_triangular_inverse(a_kk, row, col)
    e_gc = [jnp.exp(x) for x in gc]
    rhs = [jnp.concatenate([x, y * e], axis=1) for x, y, e in zip(vb, kb, e_gc)]
    sol = _each(_dot_x3, t_inv, rhs)
    qk = _each(_dot_nt, [x.astype(BF16) for x in q], k16)
    a_qk = [(x * d).astype(BF16) for x, d in zip(qk, decay)]
    gc_edge = [x[e:e + 1, :] for x, e in zip(gc, edge)]
    q_s = [(x * e).astype(BF16) for x, e in zip(q, e_gc)]
    k_tail = [(x * jnp.exp(ge - c)).astype(BF16) for x, ge, c in zip(k, gc_edge, gc)]
    s16 = [x.astype(BF16) for x in s_mat]
    ws_s = _each(_dot, [x[:, LANES:].astype(BF16) for x in sol], s16)
    u16 = [(x[:, :LANES] - y).astype(BF16) for x, y in zip(sol, ws_s)]
    o_state = _each(_dot, q_s, s16)
    o_local = _each(_dot, a_qk, u16)
    s_add = _each(_dot_tn, k_tail, u16)
    o = [x + y for x, y in zip(o_state, o_local)]
    s_new = [x * jnp.exp(ge) + y for x, ge, y in zip(s_mat, gc_edge, s_add)]
    return o, s_new


def _gdn_scan_kernel(qf_ref, kf_ref, vf_ref, gf_ref, qb_ref, kb_ref, vb_ref, gb_ref, of_ref, ob_ref, s_ref,
                     *, n_chunks, n_batch, n_heads):
    head = pl.program_id(0)

    @pl.when(pl.program_id(1) == 0)
    def _():
        s_ref[...] = jnp.zeros_like(s_ref)

    c_n = GDN_CHUNK
    row = lax.broadcasted_iota(I32, (c_n, c_n), 0)
    col = lax.broadcasted_iota(I32, (c_n, c_n), 1)
    lane = lax.broadcasted_iota(I32, (c_n, LANES), 1)
    dirs = ((qf_ref, kf_ref, vf_ref, gf_ref, of_ref, False, 0, 2 * n_heads),
            (qb_ref, kb_ref, vb_ref, gb_ref, ob_ref, True, n_heads, 3 * n_heads))

    def chunk(ci, carry):
        chains, dests = [], []
        for di, (q_ref, k_ref, v_ref, g_ref, o_ref, rev, beta0, g0) in enumerate(dirs):
            cj = (n_chunks - 1 - ci) if rev else ci
            r0 = pl.multiple_of(cj * c_n, c_n)
            for bi in range(n_batch):
                si = di * n_batch + bi
                chains.append((q_ref[bi, pl.ds(r0, c_n), :], k_ref[bi, pl.ds(r0, c_n), :],
                               v_ref[bi, pl.ds(r0, c_n), :], g_ref[bi, pl.ds(r0, c_n), :], s_ref[si],
                               rev, beta0, g0))
                dests.append((o_ref, bi, r0, si))
        o, s_new = _gdn_chunks(chains, head, row, col, lane)
        for o_c, s_c, (o_ref, bi, r0, si) in zip(o, s_new, dests):
            s_ref[si] = s_c
            o_ref[bi, pl.ds(r0, c_n), :] = o_c
        return carry

    lax.fori_loop(0, n_chunks, chunk, 0)


def _gdn_scan(qkv, gb, d_model):
    b, l, _ = qkv.shape
    n_heads = d_model // LANES
    tl = _pick(l, 512, GDN_CHUNK)
    n_l = l // tl
    kern = functools.partial(_gdn_scan_kernel, n_chunks=tl // GDN_CHUNK, n_batch=b, n_heads=n_heads)
    fwd = lambda off: pl.BlockSpec((b, tl, LANES), lambda h, i: (0, i, off + h))
    bwd = lambda off: pl.BlockSpec((b, tl, LANES), lambda h, i: (0, n_l - 1 - i, off + h))
    gate_f = pl.BlockSpec((b, tl, gb.shape[-1]), lambda h, i: (0, i, 0))
    gate_b = pl.BlockSpec((b, tl, gb.shape[-1]), lambda h, i: (0, n_l - 1 - i, 0))
    out_sd = jax.ShapeDtypeStruct((b, l, d_model), F32)
    return pl.pallas_call(
        kern,
        grid=(n_heads, n_l),
        in_specs=[fwd(0), fwd(n_heads), fwd(2 * n_heads), gate_f,
                  bwd(0), bwd(n_heads), bwd(2 * n_heads), gate_b],
        out_specs=[pl.BlockSpec((b, tl, LANES), lambda h, i: (0, i, h)),
                   pl.BlockSpec((b, tl, LANES), lambda h, i: (0, n_l - 1 - i, h))],
        out_shape=[out_sd, out_sd],
        scratch_shapes=[pltpu.VMEM((2 * b, LANES, LANES), F32)],
        compiler_params=_params("parallel", "arbitrary"),
        name="gdn_scan",
    )(qkv, qkv, qkv, gb, qkv, qkv, qkv, gb)


def _gdn_out_kernel(of_ref, ob_ref, gt_ref, w_ref, o_ref):
    o = of_ref[...] + ob_ref[...]
    ms = jnp.mean(o * o, axis=-1, keepdims=True)
    y = o * lax.rsqrt(ms + NORM_EPS) * w_ref[...]
    o_ref[...] = (y * _silu(gt_ref[...])).astype(o_ref.dtype)


def _gdn_out(o_f, o_b, proj, gate_block0, o_norm):
    b, l, d = o_f.shape
    tm = _pick(l, 1024)
    spec = pl.BlockSpec((None, tm, LANES), lambda bi, i, h: (bi, i, h))
    return pl.pallas_call(
        _gdn_out_kernel,
        grid=(b, l // tm, d // LANES),
        in_specs=[spec, spec,
                  pl.BlockSpec((None, tm, LANES), lambda bi, i, h: (bi, i, gate_block0 + h)),
                  pl.BlockSpec((1, LANES), lambda bi, i, h: (0, 0))],
        out_specs=spec,
        out_shape=jax.ShapeDtypeStruct((b, l, d), BF16),
        compiler_params=_params("parallel", "parallel", "parallel"),
        name="gdn_out",
    )(o_f, o_b, proj, o_norm.reshape(1, LANES))


def _gdn_mixer(x, g, sc, sh, gate, params):
    w_in, conv, a_log, dt_bias, o_norm, w_out = params
    d = x.shape[-1]
    n_heads = a_log.shape[1]
    assert d // n_heads == LANES and 4 * n_heads == LANES
    proj = _norm_matmul(x, g, sc, sh, w_in)
    qkv = _conv3(proj, conv, 3 * d, gdn_d=d)
    gb = _gdn_gates(proj, (4 * d) // LANES, a_log, dt_bias)
    o_f, o_b = _gdn_scan(qkv, gb, d)
    y = _gdn_out(o_f, o_b, proj, (3 * d) // LANES, o_norm)
    return _resid_matmul(y, w_out, x, gate)


def _top_rows(s, k, payload=None):
    r_n = s.shape[0]
    row = lax.broadcasted_iota(I32, s.shape, 0)
    vals, idxs, pays = [], [], []
    for _ in range(k):
        m = jnp.max(s, axis=0, keepdims=True)
        idx = jnp.min(jnp.where(s == m, row, r_n), axis=0, keepdims=True)
        hit = row == idx
        vals.append(m)
        idxs.append(idx)
        if payload is not None:
            pays.append(jnp.max(jnp.where(hit, payload, -1), axis=0, keepdims=True))
        s = jnp.where(hit, -jnp.inf, s)
    out = (jnp.concatenate(vals, axis=0), jnp.concatenate(idxs, axis=0))
    if payload is not None:
        out += (jnp.concatenate(pays, axis=0),)
    return out


def _peer_route_kernel(q_ref, khi_ref, klo_ref, ids_ref, gates_ref, *, n_heads, n_keys, key_dim):
    k_top = PEER_TOPK
    for hd in range(n_heads):
        tops = []
        for p in range(2):
            c0 = (hd * 2 + p) * key_dim
            q_hi, q_lo = _split2(q_ref[:, c0:c0 + key_dim])
            s = _dot3(khi_ref[p], klo_ref[p], q_hi, q_lo, _dot_nt)
            tops.append(_top_rows(s, k_top))
        (s0, i0), (s1, i1) = tops
        cand = jnp.concatenate([s0[a:a + 1, :] + s1 for a in range(k_top)], axis=0)
        cid = jnp.concatenate([i0[a:a + 1, :] * n_keys + i1 for a in range(k_top)], axis=0)
        best, _, ids = _top_rows(cand, k_top, payload=cid)
        e = jnp.exp(best - best[0:1, :])
        gates = e / jnp.sum(e, axis=0, keepdims=True)
        ids_ref[hd * k_top:(hd + 1) * k_top, :] = ids
        gates_ref[hd * k_top:(hd + 1) * k_top, :] = gates


def _peer_route(q, keys):
    t_n, qd = q.shape
    _, n_keys, key_dim = keys.shape
    n_heads = qd // (2 * key_dim)
    tm = _pick(t_n, 256, LANES)
    k_hi, k_lo = _split2(keys)
    kern = functools.partial(_peer_route_kernel, n_heads=n_heads, n_keys=n_keys, key_dim=key_dim)
    rows = n_heads * PEER_TOPK
    k_spec = pl.BlockSpec((2, n_keys, key_dim), lambda i: (0, 0, 0))
    o_spec = pl.BlockSpec((rows, tm), lambda i: (0, i))
    return pl.pallas_call(
        kern,
        grid=(t_n // tm,),
        in_specs=[pl.BlockSpec((tm, qd), lambda i: (i, 0)), k_spec, k_spec],
        out_specs=[o_spec, o_spec],
        out_shape=[jax.ShapeDtypeStruct((rows, t_n), I32), jax.ShapeDtypeStruct((rows, t_n), F32)],
        compiler_params=_params("parallel"),
        name="peer_route",
    )(q, k_hi, k_lo)


def _peer_gather_kernel(ids_ref, gates_ref, x_ref, g_ref, sc_ref, sh_ref, og_ref, uv_hbm,
                        o_ref, h_ref, buf, sem, *, tb, n_sel, n_slot, rows, grp):
    x = x_ref[...]
    ms = jnp.sum(jnp.sum(x * x, axis=2, keepdims=True), axis=1, keepdims=True) * (1.0 / (rows * LANES))
    h_ref[...] = (x * lax.rsqrt(ms + NORM_EPS) * g_ref[...]) * (1.0 + sc_ref[...]) + sh_ref[...]

    n_q = rows // grp
    jc_n = SUBLANES
    sel = (lax.broadcasted_iota(I32, (n_sel, n_sel * grp), 1) // grp
           == lax.broadcasted_iota(I32, (n_sel, n_sel * grp), 0)).astype(BF16)
    sel_t = (lax.broadcasted_iota(I32, (n_sel * grp, n_sel), 0) // grp
             == lax.broadcasted_iota(I32, (n_sel * grp, n_sel), 1)).astype(BF16)
    lane_t = lax.broadcasted_iota(I32, (n_sel, tb), 1)

    n_chunk = n_sel // jc_n

    def copies(t, slot):
        def start(j):
            pltpu.make_async_copy(uv_hbm.at[ids_ref[t, j]], buf.at[slot, j], sem.at[slot]).start()
        return [functools.partial(start, j) for j in range(n_sel)]

    def wait(slot):
        pltpu.make_async_copy(uv_hbm.at[pl.ds(0, n_sel)], buf.at[slot], sem.at[slot]).wait()

    def scores(t, slot, tick):
        h = h_ref[t]
        h_q = [h[qi * grp:(qi + 1) * grp] for qi in range(n_q)]
        parts = []
        for jc in range(n_chunk):
            ub = buf[slot, jc * jc_n:(jc + 1) * jc_n, 0:rows, :].astype(F32)
            q = ub[:, 0:grp] * h_q[0]
            for qi in range(1, n_q):
                q = q + ub[:, qi * grp:(qi + 1) * grp] * h_q[qi]
            parts.append(q)
            tick()
        q_hi, q_lo = _split2(jnp.concatenate(parts, axis=0).reshape(n_sel * grp, LANES))
        part = _dot(sel, q_hi) + _dot(sel, q_lo)
        act = jnp.sum(part, axis=1, keepdims=True)
        gate_col = jnp.sum(jnp.where(lane_t == t, gates_ref[...], 0.0), axis=1, keepdims=True)
        wgt = gate_col * (0.5 * act * (1.0 + lax.erf(act * (2.0 ** -0.5))))
        w_hi, w_lo = _split2(jnp.broadcast_to(wgt, (n_sel, LANES)))
        return (_dot(sel_t, w_hi) + _dot(sel_t, w_lo)).reshape(n_sel, grp, LANES)

    def combine(t, slot, w_rep, tick):
        acc = [None] * n_q
        for jc in range(n_chunk):
            vb = buf[slot, jc * jc_n:(jc + 1) * jc_n, rows:2 * rows, :].astype(F32)
            ws = w_rep[jc * jc_n:(jc + 1) * jc_n]
            for qi in range(n_q):
                term = jnp.sum(vb[:, qi * grp:(qi + 1) * grp] * ws, axis=0)
                acc[qi] = term if acc[qi] is None else acc[qi] + term
            tick()
        y = jnp.concatenate(acc, axis=0) if n_q > 1 else acc[0]
        return x_ref[t] + og_ref[...] * y

    def pair(p, slot_pair, prefetch):
        ta, tb_ = 2 * p, 2 * p + 1
        sa, sb = 2 * slot_pair, 2 * slot_pair + 1
        pending = []
        if prefetch:
            nxt_pair = (slot_pair + n_pair - 1) % n_pair
            tn = 2 * (p + n_pair - 1)
            pending = copies(tn, 2 * nxt_pair) + copies(tn + 1, 2 * nxt_pair + 1)
        per_tick = -(-len(pending) // (4 * n_chunk))

        def tick():
            for _ in range(min(per_tick, len(pending))):
                pending.pop(0)()

        wait(sa)
        wait(sb)
        w_a = scores(ta, sa, tick)
        w_b = scores(tb_, sb, tick)
        out_a = combine(ta, sa, w_a, tick)
        out_b = combine(tb_, sb, w_b, tick)
        assert not pending
        o_ref[ta] = out_a
        o_ref[tb_] = out_b

    n_pair = n_slot // 2
    n_group = (tb // 2) // n_pair
    for sp in range(n_pair - 1):
        for start in copies(2 * sp, 2 * sp) + copies(2 * sp + 1, 2 * sp + 1):
            start()

    def group(gi, carry):
        for sp in range(n_pair):
            pair(gi * n_pair + sp, sp, True)
        return carry

    lax.fori_loop(0, n_group - 1, group, 0)
    for sp in range(n_pair):
        pair((n_group - 1) * n_pair + sp, sp, sp == 0)


def _peer_gather(ids, gates_t, x, g, sc, sh, gate, u_tab, v_tab):
    b, l, d = x.shape
    t_n, n_sel = ids.shape
    rows = d // LANES
    grp = min(SUBLANES, rows)
    tb = _pick(l, PEER_TOKENS_PER_STEP, LANES)
    assert tb % PEER_SLOTS == 0 and PEER_SLOTS >= 4 and n_sel % SUBLANES == 0
    per_b = l // tb
    tiles = lambda a: a.reshape(a.shape[0], rows, LANES)
    uv = jnp.concatenate([tiles(u_tab.astype(BF16)), tiles(v_tab.astype(BF16))], axis=1)
    mod_spec = pl.BlockSpec((None, rows, LANES), lambda i: (i // per_b, 0, 0))
    kern = functools.partial(_peer_gather_kernel, tb=tb, n_sel=n_sel, n_slot=PEER_SLOTS, rows=rows, grp=grp)
    out = pl.pallas_call(
        kern,
        grid=(t_n // tb,),
        in_specs=[pl.BlockSpec((tb, n_sel), lambda i: (i, 0), memory_space=pltpu.SMEM),
                  pl.BlockSpec((n_sel, tb), lambda i: (0, i)),
                  pl.BlockSpec((tb, rows, LANES), lambda i: (i, 0, 0)),
                  pl.BlockSpec((rows, LANES), lambda i: (0, 0)),
                  mod_spec, mod_spec, mod_spec,
                  pl.BlockSpec(memory_space=pl.ANY)],
        out_specs=pl.BlockSpec((tb, rows, LANES), lambda i: (i, 0, 0)),
        out_shape=jax.ShapeDtypeStruct((t_n, rows, LANES), F32),
        scratch_shapes=[pltpu.VMEM((tb, rows, LANES), F32),
                        pltpu.VMEM((PEER_SLOTS, n_sel, 2 * rows, LANES), BF16),
                        pltpu.SemaphoreType.DMA((PEER_SLOTS,))],
        compiler_params=_params("arbitrary"),
        name="peer_gather",
    )(ids, gates_t, x.reshape(t_n, rows, LANES), g.reshape(rows, LANES), tiles(sc), tiles(sh), tiles(gate), uv)
    return out.reshape(b, l, d)


def _peer(x, g, sc, sh, gate, params):
    w_q, keys, u_tab, v_tab = params
    b, l, d = x.shape
    q = _norm_matmul(x, g, sc, sh, w_q, three_pass=True)
    ids_t, gates_t = _peer_route(q.reshape(b * l, -1), keys)
    return _peer_gather(ids_t.T, gates_t, x, g, sc, sh, gate, u_tab, v_tab)


def _final_norm_kernel(x_ref, g_ref, o_ref):
    x = x_ref[...]
    ms = jnp.mean(x * x, axis=-1, keepdims=True)
    o_ref[...] = x * lax.rsqrt(ms + NORM_EPS) * g_ref[...]


def _final_norm(x, g):
    b, l, d = x.shape
    tm = _pick(l, 256)
    return pl.pallas_call(
        _final_norm_kernel,
        grid=(b, l // tm),
        in_specs=[pl.BlockSpec((None, tm, d), lambda bi, i: (bi, i, 0)),
                  pl.BlockSpec((1, d), lambda bi, i: (0, 0))],
        out_specs=pl.BlockSpec((None, tm, d), lambda bi, i: (bi, i, 0)),
        out_shape=jax.ShapeDtypeStruct((b, l, d), F32),
        compiler_params=_params("parallel", "parallel"),
        name="final_norm",
    )(x, g.reshape(1, d))


_MIXERS = (_hyena_mixer, _pool_mixer, _gdn_mixer)


def _trunk(x, mods, layers, final_norm):
    d = x.shape[-1]
    for li, (norm_tok, norm_ch, mixer_params, peer_params) in enumerate(layers):
        mod = mods[li].reshape(x.shape[0], 1, N_MOD, d)
        sh_t, sc_t, g_t, sh_c, sc_c, g_c = (mod[:, :, m] for m in range(N_MOD))
        x = _MIXERS[li % len(_MIXERS)](x, norm_tok, sc_t, sh_t, g_t, mixer_params)
        x = _peer(x, norm_ch, sc_c, sh_c, g_c, peer_params)
    return _final_norm(x, final_norm)


def kernel(x_prompt, x_sample, c_prompt, c_sample, l0_ada_w, l0_ada_b, l0_norm_tok, l0_norm_ch, l0_hy_w_in, l0_hy_conv, l0_hy_ffn_w1, l0_hy_ffn_b1, l0_hy_ffn_w2, l0_hy_ffn_b2, l0_hy_ffn_w3, l0_hy_ffn_b3, l0_hy_sin_freq, l0_hy_log_decay, l0_hy_bias, l0_hy_w_out, l0_peer_w_q, l0_peer_keys, l0_peer_u, l0_peer_v, l1_ada_w, l1_ada_b, l1_norm_tok, l1_norm_ch, l1_pool_w_in, l1_pool_w_grp, l1_pool_scale, l1_pool_w_out, l1_peer_w_q, l1_peer_keys, l1_peer_u, l1_peer_v, l2_ada_w, l2_ada_b, l2_norm_tok, l2_norm_ch, l2_gdn_w_in, l2_gdn_conv, l2_gdn_A_log, l2_gdn_dt_bias, l2_gdn_o_norm, l2_gdn_w_out, l2_peer_w_q, l2_peer_keys, l2_peer_u, l2_peer_v, l3_ada_w, l3_ada_b, l3_norm_tok, l3_norm_ch, l3_hy_w_in, l3_hy_conv, l3_hy_ffn_w1, l3_hy_ffn_b1, l3_hy_ffn_w2, l3_hy_ffn_b2, l3_hy_ffn_w3, l3_hy_ffn_b3, l3_hy_sin_freq, l3_hy_log_decay, l3_hy_bias, l3_hy_w_out, l3_peer_w_q, l3_peer_keys, l3_peer_u, l3_peer_v, final_norm):
    layers = (
        (l0_norm_tok, l0_norm_ch,
         (l0_hy_w_in, l0_hy_conv, l0_hy_ffn_w1, l0_hy_ffn_b1, l0_hy_ffn_w2, l0_hy_ffn_b2,
          l0_hy_ffn_w3, l0_hy_ffn_b3, l0_hy_sin_freq, l0_hy_log_decay, l0_hy_bias, l0_hy_w_out),
         (l0_peer_w_q, l0_peer_keys, l0_peer_u, l0_peer_v)),
        (l1_norm_tok, l1_norm_ch,
         (l1_pool_w_in, l1_pool_w_grp, l1_pool_scale, l1_pool_w_out),
         (l1_peer_w_q, l1_peer_keys, l1_peer_u, l1_peer_v)),
        (l2_norm_tok, l2_norm_ch,
         (l2_gdn_w_in, l2_gdn_conv, l2_gdn_A_log, l2_gdn_dt_bias, l2_gdn_o_norm, l2_gdn_w_out),
         (l2_peer_w_q, l2_peer_keys, l2_peer_u, l2_peer_v)),
        (l3_norm_tok, l3_norm_ch,
         (l3_hy_w_in, l3_hy_conv, l3_hy_ffn_w1, l3_hy_ffn_b1, l3_hy_ffn_w2, l3_hy_ffn_b2,
          l3_hy_ffn_w3, l3_hy_ffn_b3, l3_hy_sin_freq, l3_hy_log_decay, l3_hy_bias, l3_hy_w_out),
         (l3_peer_w_q, l3_peer_keys, l3_peer_u, l3_peer_v)),
    )
    ada = ((l0_ada_w, l0_ada_b), (l1_ada_w, l1_ada_b), (l2_ada_w, l2_ada_b), (l3_ada_w, l3_ada_b))
    n_p = c_prompt.shape[0]
    n_s = c_sample.shape[0]
    pad = (-(n_p + n_s)) % SUBLANES
    c_rows = jnp.concatenate([c_prompt, c_sample, jnp.zeros((pad, c_prompt.shape[1]), F32)], axis=0)
    mods = [_ada(c_rows, w, b) for w, b in ada]
    y_prompt = _trunk(x_prompt, [m[:n_p] for m in mods], layers, final_norm)
    y_sample = _trunk(x_sample, [m[n_p:n_p + n_s] for m in mods], layers, final_norm)
    return (y_prompt, y_sample)
```

```python
import functools
import math

import jax
import jax.numpy as jnp
from jax import lax
from jax.experimental import pallas as pl
from jax.experimental.pallas import tpu as pltpu

F32 = jnp.float32
BF16 = jnp.bfloat16
I32 = jnp.int32
HIGHEST = lax.Precision.HIGHEST

NORM_EPS = 1e-6
N_MOD = 6
LANES = 128
SUBLANES = 8
VMEM_LIMIT_BYTES = 56 * 1024 * 1024
FFT_N2 = 128
POOL_WINDOWS = (2, 4, 8, 16)
POOL_HALO = 8
GDN_CHUNK = 64
PEER_TOPK = 16
PEER_TOKENS_PER_STEP = 128
PEER_SLOTS = 8


def _pick(n, pref, mult=SUBLANES):
    t = (min(pref, n) // mult) * mult
    while t >= mult:
        if n % t == 0:
            return t
        t -= mult
    return n


def _params(*sem):
    return pltpu.CompilerParams(dimension_semantics=sem, vmem_limit_bytes=VMEM_LIMIT_BYTES)


def _split2(x):
    hi = x.astype(BF16)
    lo = (x - hi.astype(F32)).astype(BF16)
    return hi, lo


def _split3(x):
    hi = x.astype(BF16)
    r = x - hi.astype(F32)
    mid = r.astype(BF16)
    lo = (r - mid.astype(F32)).astype(BF16)
    return hi, mid, lo


def _dot(a, b):
    return jnp.dot(a, b, preferred_element_type=F32)


def _dot_nt(a, b):
    return lax.dot_general(a, b, (((1,), (1,)), ((), ())), preferred_element_type=F32)


def _dot_tn(a, b):
    return lax.dot_general(a, b, (((0,), (0,)), ((), ())), preferred_element_type=F32)


def _dot3(a_hi, a_lo, b_hi, b_lo, dot=_dot):
    return dot(a_hi, b_hi) + dot(a_lo, b_hi) + dot(a_hi, b_lo)


def _dot_x3(a, b, dot=_dot):
    a_hi, a_lo = _split2(a)
    b_hi, b_lo = _split2(b)
    return _dot3(a_hi, a_lo, b_hi, b_lo, dot)


def _dot_exact_lhs(a_exact, b):
    a = a_exact.astype(BF16)
    b_hi, b_mid, b_lo = _split3(b)
    return _dot(a, b_hi) + _dot(a, b_mid) + _dot(a, b_lo)


def _silu(x):
    return x * jax.nn.sigmoid(x)


def _ada_kernel(c_ref, w_ref, b_ref, o_ref):
    a = _silu(c_ref[...])
    o_ref[...] = jnp.dot(a, w_ref[...], precision=HIGHEST, preferred_element_type=F32) + b_ref[...]


def _ada(c_rows, w, b):
    rows, d = c_rows.shape
    n = w.shape[1]
    tn = _pick(n, 512, LANES)
    return pl.pallas_call(
        _ada_kernel,
        grid=(n // tn,),
        in_specs=[pl.BlockSpec((rows, d), lambda j: (0, 0)),
                  pl.BlockSpec((d, tn), lambda j: (0, j)),
                  pl.BlockSpec((1, tn), lambda j: (0, j))],
        out_specs=pl.BlockSpec((rows, tn), lambda j: (0, j)),
        out_shape=jax.ShapeDtypeStruct((rows, n), F32),
        compiler_params=_params("parallel"),
        name="ada_mod",
    )(c_rows, w, b.reshape(1, n))


def _modulated_norm(x, g, sc, sh):
    ms = jnp.mean(x * x, axis=-1, keepdims=True)
    return (x * lax.rsqrt(ms + NORM_EPS) * g) * (1.0 + sc) + sh


def _nm_kernel(x_ref, g_ref, sc_ref, sh_ref, w_ref, o_ref, h_ref):
    @pl.when(pl.program_id(2) == 0)
    def _():
        h_ref[...] = _modulated_norm(x_ref[...], g_ref[...], sc_ref[...], sh_ref[...]).astype(BF16)

    o_ref[...] = _dot(h_ref[...], w_ref[...]).astype(o_ref.dtype)


def _nm3_kernel(x_ref, g_ref, sc_ref, sh_ref, whi_ref, wlo_ref, o_ref, hhi_ref, hlo_ref):
    @pl.when(pl.program_id(2) == 0)
    def _():
        hi, lo = _split2(_modulated_norm(x_ref[...], g_ref[...], sc_ref[...], sh_ref[...]))
        hhi_ref[...] = hi
        hlo_ref[...] = lo

    o_ref[...] = _dot3(hhi_ref[...], hlo_ref[...], whi_ref[...], wlo_ref[...]).astype(o_ref.dtype)


def _norm_matmul(x, g, sc, sh, w, three_pass=False, out_dtype=F32):
    b, l, d = x.shape
    n = w.shape[1]
    tm = _pick(l, 256 if three_pass else 512)
    tn = _pick(n, 512, LANES)
    x_spec = pl.BlockSpec((None, tm, d), lambda bi, i, j: (bi, i, 0))
    g_spec = pl.BlockSpec((1, d), lambda bi, i, j: (0, 0))
    m_spec = pl.BlockSpec((None, 1, d), lambda bi, i, j: (bi, 0, 0))
    w_spec = pl.BlockSpec((d, tn), lambda bi, i, j: (0, j))
    o_spec = pl.BlockSpec((None, tm, tn), lambda bi, i, j: (bi, i, j))
    if three_pass:
        w_hi, w_lo = _split2(w)
        kern, w_args, w_specs = _nm3_kernel, (w_hi, w_lo), [w_spec, w_spec]
        scratch = [pltpu.VMEM((tm, d), BF16), pltpu.VMEM((tm, d), BF16)]
    else:
        kern, w_args, w_specs = _nm_kernel, (w.astype(BF16),), [w_spec]
        scratch = [pltpu.VMEM((tm, d), BF16)]
    return pl.pallas_call(
        kern,
        grid=(b, l // tm, n // tn),
        in_specs=[x_spec, g_spec, m_spec, m_spec] + w_specs,
        out_specs=o_spec,
        out_shape=jax.ShapeDtypeStruct((b, l, n), out_dtype),
        scratch_shapes=scratch,
        compiler_params=_params("parallel", "parallel", "arbitrary"),
        name="norm_matmul3" if three_pass else "norm_matmul",
    )(x, g.reshape(1, d), sc, sh, *w_args)


def _resid_mm_kernel(z_ref, w_ref, x_ref, gt_ref, o_ref):
    o_ref[...] = x_ref[...] + gt_ref[...] * _dot(z_ref[...].astype(BF16), w_ref[...])


def _resid_matmul(z, w, x, gate):
    b, l, k = z.shape
    n = w.shape[1]
    tm = _pick(l, 512)
    tn = _pick(n, 512, LANES)
    return pl.pallas_call(
        _resid_mm_kernel,
        grid=(b, l // tm, n // tn),
        in_specs=[pl.BlockSpec((None, tm, k), lambda bi, i, j: (bi, i, 0)),
                  pl.BlockSpec((k, tn), lambda bi, i, j: (0, j)),
                  pl.BlockSpec((None, tm, tn), lambda bi, i, j: (bi, i, j)),
                  pl.BlockSpec((None, 1, tn), lambda bi, i, j: (bi, 0, j))],
        out_specs=pl.BlockSpec((None, tm, tn), lambda bi, i, j: (bi, i, j)),
        out_shape=jax.ShapeDtypeStruct((b, l, n), F32),
        compiler_params=_params("parallel", "parallel", "parallel"),
        name="resid_matmul",
    )(z, w.astype(BF16), x, gate)


def _conv3_kernel(x_ref, p_ref, n_ref, w_ref, o_ref, *, tm, gdn_d, head_scale):
    i = pl.program_id(1)
    last = pl.num_programs(1) - 1
    x = x_ref[...]
    rows = lax.broadcasted_iota(I32, (tm, 1), 0)
    prev_row = jnp.where(i > 0, p_ref[SUBLANES - 1:SUBLANES, :], 0.0)
    next_row = jnp.where(i < last, n_ref[0:1, :], 0.0)
    x_m = jnp.where(rows == 0, prev_row, pltpu.roll(x, 1, axis=0))
    x_p = jnp.where(rows == tm - 1, next_row, pltpu.roll(x, tm - 1, axis=0))
    w = w_ref[...]
    y = x_m * w[0:1, :] + x * w[1:2, :] + x_p * w[2:3, :]
    if not gdn_d:
        o_ref[...] = y
        return
    y = _silu(y)
    sec = (pl.program_id(2) * y.shape[1]) // gdn_d
    for hd in range(y.shape[1] // LANES):
        y_h = y[:, hd * LANES:(hd + 1) * LANES]
        nrm = y_h * lax.rsqrt(jnp.sum(y_h * y_h, axis=-1, keepdims=True) + NORM_EPS)
        nrm = nrm * jnp.where(sec == 0, head_scale, 1.0)
        o_ref[:, hd * LANES:(hd + 1) * LANES] = jnp.where(sec < 2, nrm, y_h)


def _conv3(x, w, n_cols, gdn_d=0):
    b, l, _ = x.shape
    tm = _pick(l, 512)
    tc = _pick(gdn_d if gdn_d else n_cols, 1024, LANES)
    r8 = tm // SUBLANES
    kern = functools.partial(_conv3_kernel, tm=tm, gdn_d=gdn_d, head_scale=float(LANES) ** -0.5)
    return pl.pallas_call(
        kern,
        grid=(b, l // tm, n_cols // tc),
        in_specs=[pl.BlockSpec((None, tm, tc), lambda bi, i, c: (bi, i, c)),
                  pl.BlockSpec((None, SUBLANES, tc), lambda bi, i, c: (bi, jnp.maximum(i * r8 - 1, 0), c)),
                  pl.BlockSpec((None, SUBLANES, tc),
                               lambda bi, i, c: (bi, jnp.minimum((i + 1) * r8, l // SUBLANES - 1), c)),
                  pl.BlockSpec((3, tc), lambda bi, i, c: (0, c))],
        out_specs=pl.BlockSpec((None, tm, tc), lambda bi, i, c: (bi, i, c)),
        out_shape=jax.ShapeDtypeStruct((b, l, n_cols), F32),
        compiler_params=_params("parallel", "parallel", "parallel"),
        name="conv3_gdn" if gdn_d else "conv3",
    )(x, x, x, w)


def _hyfilt_kernel(band_ref, w1_ref, b1_ref, w2_ref, b2_ref, sf_ref, w3_ref, b3_ref, ld_ref,
                   o_ref, ss_ref, *, tl, seq, n_band):
    i = pl.program_id(1)
    r = i * tl + lax.broadcasted_iota(I32, (tl, 1), 0)
    pos = jnp.where(r < seq, r, 2 * seq - r).astype(F32)
    t = pos / float(seq - 1)
    omega = (2.0 * math.pi) * pos / float(seq)
    ang = omega * band_ref[...]
    lane = lax.broadcasted_iota(I32, (tl, LANES), 1)
    feats = jnp.where(lane == 0, t,
                      jnp.where(lane <= n_band, jnp.cos(ang),
                                jnp.where(lane <= 2 * n_band, -jnp.sin(ang), 0.0)))
    sf = sf_ref[...]
    h = jnp.sin(sf[0:1, :] * (jnp.dot(feats, w1_ref[...], precision=HIGHEST) + b1_ref[...]))
    h = jnp.sin(sf[1:2, :] * (jnp.dot(h, w2_ref[...], precision=HIGHEST) + b2_ref[...]))
    h = jnp.dot(h, w3_ref[...], precision=HIGHEST) + b3_ref[...]
    out = jnp.where(r == seq, 0.0, h * jnp.exp(-t * jnp.exp(ld_ref[...])))
    o_ref[...] = out

    @pl.when(i == 0)
    def _():
        ss_ref[...] = jnp.zeros_like(ss_ref)

    ss_ref[...] += jnp.sum(out * out, axis=0, keepdims=True)


def _hyena_filters(seq, w1, b1, w2, b2, w3, b3, sin_freq, log_decay):
    emb, hid = w1.shape
    n_band = (emb - 1) // 2
    n_order, n_dir, d_model = log_decay.shape
    assert n_dir == 2
    n_out = n_order * d_model
    bands = jnp.linspace(1e-4, n_band - 1, n_band, dtype=F32)
    band_row = jnp.zeros((1, LANES), F32).at[0, 1:1 + n_band].set(bands).at[0, 1 + n_band:1 + 2 * n_band].set(bands)
    w1p = jnp.zeros((LANES, hid), F32).at[:emb].set(w1)
    tl = _pick(seq, 512)
    tn = _pick(d_model, 1024, LANES)
    per_d = d_model // tn
    kern = functools.partial(_hyfilt_kernel, tl=tl, seq=seq, n_band=n_band)
    full = lambda shape: pl.BlockSpec(shape, lambda j, i: (0,) * len(shape))
    src_col = lambda j, i: (0, ((j // per_d) * 2 + (i * tl) // seq) * per_d + j % per_d)
    return pl.pallas_call(
        kern,
        grid=(n_out // tn, 2 * seq // tl),
        in_specs=[full((1, LANES)), full((LANES, hid)), full((1, hid)), full((hid, hid)), full((1, hid)),
                  full((2, hid)),
                  pl.BlockSpec((hid, tn), src_col),
                  pl.BlockSpec((1, tn), src_col),
                  pl.BlockSpec((1, tn), src_col)],
        out_specs=[pl.BlockSpec((tl, tn), lambda j, i: (i, j)),
                   pl.BlockSpec((1, tn), lambda j, i: (0, j))],
        out_shape=[jax.ShapeDtypeStruct((2 * seq, n_out), F32), jax.ShapeDtypeStruct((1, n_out), F32)],
        compiler_params=_params("parallel", "arbitrary"),
        name="hyena_filters",
    )(band_row, w1p, b1.reshape(1, hid), w2, b2.reshape(1, hid), sin_freq, w3, b3.reshape(1, -1),
      log_decay.reshape(1, -1))


def _lmm_kernel(mh_ref, ml_ref, x_ref, o_ref, *, gb):
    for g in range(gb):
        x_hi, x_lo = _split2(x_ref[g])
        o_ref[g] = _dot3(mh_ref[g], ml_ref[g], x_hi, x_lo)


def _lmm_epi_kernel(mh_ref, ml_ref, x_ref, z_ref, gt_ref, bias_ref, o_ref, *, gb):
    for g in range(gb):
        x_hi, x_lo = _split2(x_ref[g])
        conv = _dot3(mh_ref[g], ml_ref[g], x_hi, x_lo)
        z = z_ref[g]
        o_ref[g] = gt_ref[g] * (conv + z * bias_ref[...])


def _fft_stage1(mats, x, epilogue=None):
    m_hi, m_lo = mats
    g_n, r_out, r_in = m_hi.shape
    d = x.shape[-1]
    gb = _pick(g_n, 4, 1)
    td = _pick(d, 1024, LANES)
    m_spec = pl.BlockSpec((gb, r_out, r_in), lambda g, j: (g, 0, 0))
    x_spec = pl.BlockSpec((gb, r_in, td), lambda g, j: (g, 0, j))
    o_spec = pl.BlockSpec((gb, r_out, td), lambda g, j: (g, 0, j))
    if epilogue is None:
        kern, extra, extra_specs = functools.partial(_lmm_kernel, gb=gb), (), []
    else:
        kern = functools.partial(_lmm_epi_kernel, gb=gb)
        extra = epilogue
        extra_specs = [o_spec, o_spec, pl.BlockSpec((1, td), lambda g, j: (0, j))]
    return pl.pallas_call(
        kern,
        grid=(g_n // gb, d // td),
        in_specs=[m_spec, m_spec, x_spec] + extra_specs,
        out_specs=o_spec,
        out_shape=jax.ShapeDtypeStruct((g_n, r_out, d), F32),
        compiler_params=_params("parallel", "parallel"),
        name="fft_stage1" if epilogue is None else "fft_stage1_inv",
    )(m_hi, m_lo, x, *extra)


def _s2f_kernel(mh_ref, ml_ref, x_ref, o_ref, *, gb):
    for g in range(gb):
        x_hi, x_lo = _split2(x_ref[g])
        o_ref[g] = _dot3(mh_ref[...], ml_ref[...], x_hi, x_lo)


def _s2c_kernel(mh_ref, ml_ref, ih_ref, il_ref, x_ref, f_ref, ss_ref, o_ref, *, gb, half):
    scale = lax.rsqrt(ss_ref[...] + NORM_EPS)
    for g in range(gb):
        x_hi, x_lo = _split2(x_ref[g])
        spec = _dot3(mh_ref[...], ml_ref[...], x_hi, x_lo)
        f = f_ref[g]
        s_re, s_im = spec[:half], spec[half:]
        f_re, f_im = f[:half], f[half:]
        prod = jnp.concatenate([s_re * f_re - s_im * f_im, s_re * f_im + s_im * f_re], axis=0) * scale
        p_hi, p_lo = _split2(prod)
        o_ref[g] = _dot3(ih_ref[...], il_ref[...], p_hi, p_lo)


def _fft_stage2_fwd(mats, x):
    m_hi, m_lo = mats
    r = m_hi.shape[0]
    g_n, _, d = x.shape
    gb = _pick(g_n, 4, 1)
    td = _pick(d, 1024, LANES)
    m_spec = pl.BlockSpec((r, r), lambda g, j: (0, 0))
    x_spec = pl.BlockSpec((gb, r, td), lambda g, j: (g, 0, j))
    return pl.pallas_call(
        functools.partial(_s2f_kernel, gb=gb),
        grid=(g_n // gb, d // td),
        in_specs=[m_spec, m_spec, x_spec],
        out_specs=x_spec,
        out_shape=jax.ShapeDtypeStruct(x.shape, F32),
        compiler_params=_params("parallel", "parallel"),
        name="fft_stage2_fwd",
    )(m_hi, m_lo, x)


def _fft_stage2_conv(mats, imats, x, filt_spec, sumsq):
    m_hi, m_lo = mats
    i_hi, i_lo = imats
    r = m_hi.shape[0]
    g_n, _, d = x.shape
    gb = _pick(g_n, 4, 1)
    td = _pick(d, 1024, LANES)
    m_spec = pl.BlockSpec((r, r), lambda g, j: (0, 0))
    x_spec = pl.BlockSpec((gb, r, td), lambda g, j: (g, 0, j))
    return pl.pallas_call(
        functools.partial(_s2c_kernel, gb=gb, half=r // 2),
        grid=(g_n // gb, d // td),
        in_specs=[m_spec, m_spec, m_spec, m_spec, x_spec, x_spec, pl.BlockSpec((1, td), lambda g, j: (0, j))],
        out_specs=x_spec,
        out_shape=jax.ShapeDtypeStruct(x.shape, F32),
        compiler_params=_params("parallel", "parallel"),
        name="fft_stage2_conv",
    )(m_hi, m_lo, i_hi, i_lo, x, filt_spec, sumsq)


def _dft_tables(seq):
    n = 2 * seq
    n2 = FFT_N2
    n1 = n // n2
    n1h = n1 // 2
    k1 = jnp.arange(n1, dtype=I32)
    n2i = jnp.arange(n2, dtype=I32)
    n1i = jnp.arange(n1, dtype=I32)
    pos = n1i[None, :] * n2 + n2i[:, None]
    prod = (k1[None, :, None] * pos[:, None, :]) % n
    ang = prod.astype(F32) * (2.0 * math.pi / n)
    c, s = jnp.cos(ang), jnp.sin(ang)
    ch, sh = c[:, :, :n1h], s[:, :, :n1h]
    fwd_data = jnp.concatenate([jnp.concatenate([ch, sh], axis=2),
                                jnp.concatenate([-sh, ch], axis=2)], axis=1)
    fwd_real = jnp.concatenate([c, -s], axis=1)
    ct, st = jnp.swapaxes(ch, 1, 2), jnp.swapaxes(sh, 1, 2)
    inv_data = jnp.concatenate([jnp.concatenate([ct, -st], axis=2),
                                jnp.concatenate([st, ct], axis=2)], axis=1) * (1.0 / n)
    a2 = ((n2i[:, None] * n2i[None, :]) % n2).astype(F32) * (2.0 * math.pi / n2)
    c2, s2 = jnp.cos(a2), jnp.sin(a2)
    m2 = jnp.concatenate([jnp.concatenate([c2, s2], axis=1), jnp.concatenate([-s2, c2], axis=1)], axis=0)
    m2i = jnp.concatenate([jnp.concatenate([c2, -s2], axis=1), jnp.concatenate([s2, c2], axis=1)], axis=0)
    return dict(n1=n1, n1h=n1h, fwd_data=_split2(fwd_data), fwd_real=_split2(fwd_real),
                inv_data=_split2(inv_data), m2=_split2(m2), m2i=_split2(m2i))


def _to_stage2_layout(a, n1):
    n2, r, d = a.shape
    c = r // n1
    return a.reshape(n2, c, n1, d).transpose(2, 1, 0, 3).reshape(n1, c * n2, d)


def _to_stage1_layout(a, n2):
    n1, r, d = a.shape
    c = r // n2
    return a.reshape(n1, c, n2, d).transpose(2, 1, 0, 3).reshape(n2, c * n1, d)


def _hyena_core(u, two_sided, sumsq, bias):
    b, seq, d3 = u.shape
    d = d3 // 3
    n_order = bias.shape[0]
    assert b == 2
    tab = _dft_tables(seq)
    n1, n1h, n2 = tab["n1"], tab["n1h"], FFT_N2
    ut = u.reshape(b, n1h, n2, 3, d).transpose(3, 2, 0, 1, 4).reshape(3, n2, b * n1h, d)
    f_t = two_sided.reshape(n1, n2, n_order, d).transpose(2, 1, 0, 3)
    z_t = ut[0]
    for o in range(n_order):
        f_spec = _fft_stage2_fwd(tab["m2"], _to_stage2_layout(_fft_stage1(tab["fwd_real"], f_t[o]), n1))
        ss = sumsq.reshape(n_order, 1, d)[o]
        a = _to_stage2_layout(_fft_stage1(tab["fwd_data"], z_t), n1)
        c = _to_stage1_layout(_fft_stage2_conv(tab["m2"], tab["m2i"], a, f_spec, ss), n2)
        z_t = _fft_stage1(tab["inv_data"], c, epilogue=(z_t, ut[1 + o], bias[o].reshape(1, d)))
    return z_t.reshape(n2, b, n1h, d).transpose(1, 2, 0, 3).reshape(b, seq, d)


def _hyena_mixer(x, g, sc, sh, gate, params):
    w_in, conv, w1, b1, w2, b2, w3, b3, sin_freq, log_decay, bias, w_out = params
    seq = x.shape[1]
    y = _norm_matmul(x, g, sc, sh, w_in)
    u = _conv3(y, conv, y.shape[-1])
    filt, sumsq = _hyena_filters(seq, w1, b1, w2, b2, w3, b3, sin_freq, log_decay)
    z = _hyena_core(u, filt, sumsq, bias)
    return _resid_matmul(z, w_out, x, gate)


def _pool_kernel(x_ref, p_ref, n_ref, o_ref, *, tm, seq, group):
    i = pl.program_id(1)
    last = pl.num_programs(1) - 1
    x = x_ref[...]
    prev = jnp.where(i > 0, p_ref[...], 0.0)
    nxt = jnp.where(i < last, n_ref[...], 0.0)
    ext = jnp.concatenate([prev, x, nxt], axis=0)
    gi = (pl.program_id(2) * x.shape[1]) // group
    half = jnp.left_shift(1, gi)
    r = lax.broadcasted_iota(I32, (tm, tm + 2 * POOL_HALO), 0)
    c = lax.broadcasted_iota(I32, (tm, tm + 2 * POOL_HALO), 1)
    band = jnp.logical_and(c >= r + POOL_HALO - half, c < r + POOL_HALO + half).astype(F32)
    win = _dot_exact_lhs(band, ext)
    t = i * tm + lax.broadcasted_iota(I32, (tm, 1), 0)
    cnt = (jnp.minimum(t + half, seq) - jnp.maximum(t - half, 0)).astype(F32)
    o_ref[...] = win / cnt - x


def _pool_windows(uf):
    b, l, d = uf.shape
    group = d // len(POOL_WINDOWS)
    tm = _pick(l, 256)
    tc = _pick(group, 512, LANES)
    r8 = tm // SUBLANES
    return pl.pallas_call(
        functools.partial(_pool_kernel, tm=tm, seq=l, group=group),
        grid=(b, l // tm, d // tc),
        in_specs=[pl.BlockSpec((None, tm, tc), lambda bi, i, c: (bi, i, c)),
                  pl.BlockSpec((None, SUBLANES, tc), lambda bi, i, c: (bi, jnp.maximum(i * r8 - 1, 0), c)),
                  pl.BlockSpec((None, SUBLANES, tc),
                               lambda bi, i, c: (bi, jnp.minimum((i + 1) * r8, l // SUBLANES - 1), c))],
        out_specs=pl.BlockSpec((None, tm, tc), lambda bi, i, c: (bi, i, c)),
        out_shape=jax.ShapeDtypeStruct((b, l, d), F32),
        compiler_params=_params("parallel", "parallel", "parallel"),
        name="pool_windows",
    )(uf, uf, uf)


def _group_mm_kernel(p_ref, w_ref, s_ref, o_ref):
    o_ref[...] = (_dot(p_ref[...].astype(BF16), w_ref[...]) * s_ref[...]).astype(o_ref.dtype)


def _group_matmul(p, w_grp, scale):
    b, l, d = p.shape
    n_g, gd, _ = w_grp.shape
    tm = _pick(l, 512)
    tn = _pick(gd, 512, LANES)
    per = gd // tn
    return pl.pallas_call(
        _group_mm_kernel,
        grid=(b, l // tm, n_g, per),
        in_specs=[pl.BlockSpec((None, tm, gd), lambda bi, i, g, j: (bi, i, g)),
                  pl.BlockSpec((None, gd, tn), lambda bi, i, g, j: (g, 0, j)),
                  pl.BlockSpec((1, tn), lambda bi, i, g, j: (0, g * per + j))],
        out_specs=pl.BlockSpec((None, tm, tn), lambda bi, i, g, j: (bi, i, g * per + j)),
        out_shape=jax.ShapeDtypeStruct((b, l, d), BF16),
        compiler_params=_params("parallel", "parallel", "parallel", "parallel"),
        name="pool_group_matmul",
    )(p, w_grp.astype(BF16), scale.reshape(1, d))


def _pool_mixer(x, g, sc, sh, gate, params):
    w_in, w_grp, scale, w_out = params
    uf = _norm_matmul(x, g, sc, sh, w_in)
    y = _group_matmul(_pool_windows(uf), w_grp, scale)
    return _resid_matmul(y, w_out, x, gate)


def _gdn_gates_kernel(x_ref, alog_ref, dtb_ref, o_ref, *, n_heads):
    x = x_ref[...]
    lane = lax.broadcasted_iota(I32, x.shape, 1)
    decay = -jnp.exp(alog_ref[...]) * jax.nn.softplus(x + dtb_ref[...])
    o_ref[...] = jnp.where(lane < 2 * n_heads, jax.nn.sigmoid(x), decay)


def _gdn_gates(proj, col_block, a_log, dt_bias):
    b, l, _ = proj.shape
    n_heads = a_log.shape[1]
    pad = jnp.zeros((1, 2 * n_heads), F32)
    alog_row = jnp.concatenate([pad, a_log.reshape(1, 2 * n_heads)], axis=1)
    dtb_row = jnp.concatenate([pad, dt_bias.reshape(1, 2 * n_heads)], axis=1)
    tm = _pick(l, 1024)
    w = 4 * n_heads
    return pl.pallas_call(
        functools.partial(_gdn_gates_kernel, n_heads=n_heads),
        grid=(b, l // tm),
        in_specs=[pl.BlockSpec((None, tm, w), lambda bi, i: (bi, i, col_block)),
                  pl.BlockSpec((1, w), lambda bi, i: (0, 0)),
                  pl.BlockSpec((1, w), lambda bi, i: (0, 0))],
        out_specs=pl.BlockSpec((None, tm, w), lambda bi, i: (bi, i, 0)),
        out_shape=jax.ShapeDtypeStruct((b, l, w), F32),
        compiler_params=_params("parallel", "parallel"),
        name="gdn_gates",
    )(proj, alog_row, dtb_row)


def _unit_triangular_inverse(a, row, col):
    n = a[0].shape[0]
    eye = (row == col).astype(F32)
    blk = lambda s: (row // s) == (col // s)
    a8 = [jnp.where(blk(8), x, 0.0) for x in a]
    a8_2 = _each(_dot_x3, a8, a8)
    a8_4 = _each(_dot_x3, a8_2, a8_2)
    t = _each(_dot_x3, [eye - x for x in a8], [eye + x for x in a8_2])
    t = _each(_dot_x3, t, [eye + x for x in a8_4])
    s = 8
    while s < n:
        mask = jnp.logical_and(blk(2 * s), jnp.logical_not(blk(s)))
        off = [jnp.where(mask, x, 0.0) for x in a]
        corr = _each(_dot_x3, _each(_dot_x3, t, off), t)
        t = [x - y for x, y in zip(t, corr)]
        s *= 2
    return t


def _each(fn, *lists):
    return [fn(*xs) for xs in zip(*lists)]


def _gdn_chunks(chains, head, row, col, lane):
    c_n = GDN_CHUNK
    q, k, v, gb, s_mat, rev, beta0, g0 = (list(z) for z in zip(*chains))
    incl = [(row <= col) if r else (row >= col) for r in rev]
    strict = [(row < col) if r else (row > col) for r in rev]
    edge = [0 if r else c_n - 1 for r in rev]
    eye = row == col
    ones = jnp.ones((c_n, c_n), F32)
    beta = [jnp.sum(jnp.where(lane == b0 + head, x, 0.0), axis=1, keepdims=True) for x, b0 in zip(gb, beta0)]
    g = [jnp.sum(jnp.where(lane == b0 + head, x, 0.0), axis=1, keepdims=True) for x, b0 in zip(gb, g0)]
    gc = [_dot_exact_lhs(m.astype(F32), jnp.broadcast_to(x, (c_n, LANES))) for m, x in zip(incl, g)]
    gc_col = [x[:, :c_n] for x in gc]
    gc_row = [_dot_exact_lhs(ones, jnp.where(eye, x, 0.0)) for x in gc_col]
    decay = [jnp.where(m, jnp.exp(jnp.where(m, c - r, 0.0)), 0.0) for m, c, r in zip(incl, gc_col, gc_row)]
    kb = [x * b for x, b in zip(k, beta)]
    vb = [x * b for x, b in zip(v, beta)]
    k16 = [x.astype(BF16) for x in k]
    kk = _each(_dot_nt, [x.astype(BF16) for x in kb], k16)
    a_kk = [jnp.where(m, x * d, 0.0) for m, x, d in zip(strict, kk, decay)]
    t_inv = _unit_triangular_inverse(a_kk, row, col)
    e_gc = [jnp.exp(x) for x in gc]
    rhs = [jnp.concatenate([x, y * e], axis=1) for x, y, e in zip(vb, kb, e_gc)]
    sol = _each(_dot_x3, t_inv, rhs)
    qk = _each(_dot_nt, [x.astype(BF16) for x in q], k16)
    a_qk = [(x * d).astype(BF16) for x, d in zip(qk, decay)]
    gc_edge = [x[e:e + 1, :] for x, e in zip(gc, edge)]
    q_s = [(x * e).astype(BF16) for x, e in zip(q, e_gc)]
    k_tail = [(x * jnp.exp(ge - c)).astype(BF16) for x, ge, c in zip(k, gc_edge, gc)]
    s16 = [x.astype(BF16) for x in s_mat]
    ws_s = _each(_dot, [x[:, LANES:].astype(BF16) for x in sol], s16)
    u16 = [(x[:, :LANES] - y).astype(BF16) for x, y in zip(sol, ws_s)]
    o_state = _each(_dot, q_s, s16)
    o_local = _each(_dot, a_qk, u16)
    s_add = _each(_dot_tn, k_tail, u16)
    o = [x + y for x, y in zip(o_state, o_local)]
    s_new = [x * jnp.exp(ge) + y for x, ge, y in zip(s_mat, gc_edge, s_add)]
    return o, s_new


def _gdn_scan_kernel(qf_ref, kf_ref, vf_ref, gf_ref, qb_ref, kb_ref, vb_ref, gb_ref, of_ref, ob_ref, s_ref,
                     *, n_chunks, n_batch, n_heads):
    head = pl.program_id(0)

    @pl.when(pl.program_id(1) == 0)
    def _():
        s_ref[...] = jnp.zeros_like(s_ref)

    c_n = GDN_CHUNK
    row = lax.broadcasted_iota(I32, (c_n, c_n), 0)
    col = lax.broadcasted_iota(I32, (c_n, c_n), 1)
    lane = lax.broadcasted_iota(I32, (c_n, LANES), 1)
    dirs = ((qf_ref, kf_ref, vf_ref, gf_ref, of_ref, False, 0, 2 * n_heads),
            (qb_ref, kb_ref, vb_ref, gb_ref, ob_ref, True, n_heads, 3 * n_heads))

    def chunk(ci, carry):
        chains, dests = [], []
        for di, (q_ref, k_ref, v_ref, g_ref, o_ref, rev, beta0, g0) in enumerate(dirs):
            cj = (n_chunks - 1 - ci) if rev else ci
            r0 = pl.multiple_of(cj * c_n, c_n)
            for bi in range(n_batch):
                si = di * n_batch + bi
                chains.append((q_ref[bi, pl.ds(r0, c_n), :], k_ref[bi, pl.ds(r0, c_n), :],
                               v_ref[bi, pl.ds(r0, c_n), :], g_ref[bi, pl.ds(r0, c_n), :], s_ref[si],
                               rev, beta0, g0))
                dests.append((o_ref, bi, r0, si))
        o, s_new = _gdn_chunks(chains, head, row, col, lane)
        for o_c, s_c, (o_ref, bi, r0, si) in zip(o, s_new, dests):
            s_ref[si] = s_c
            o_ref[bi, pl.ds(r0, c_n), :] = o_c
        return carry

    lax.fori_loop(0, n_chunks, chunk, 0)


def _gdn_scan(qkv, gb, d_model):
    b, l, _ = qkv.shape
    n_heads = d_model // LANES
    tl = _pick(l, 512, GDN_CHUNK)
    n_l = l // tl
    kern = functools.partial(_gdn_scan_kernel, n_chunks=tl // GDN_CHUNK, n_batch=b, n_heads=n_heads)
    fwd = lambda off: pl.BlockSpec((b, tl, LANES), lambda h, i: (0, i, off + h))
    bwd = lambda off: pl.BlockSpec((b, tl, LANES), lambda h, i: (0, n_l - 1 - i, off + h))
    gate_f = pl.BlockSpec((b, tl, gb.shape[-1]), lambda h, i: (0, i, 0))
    gate_b = pl.BlockSpec((b, tl, gb.shape[-1]), lambda h, i: (0, n_l - 1 - i, 0))
    out_sd = jax.ShapeDtypeStruct((b, l, d_model), F32)
    return pl.pallas_call(
        kern,
        grid=(n_heads, n_l),
        in_specs=[fwd(0), fwd(n_heads), fwd(2 * n_heads), gate_f,
                  bwd(0), bwd(n_heads), bwd(2 * n_heads), gate_b],
        out_specs=[pl.BlockSpec((b, tl, LANES), lambda h, i: (0, i, h)),
                   pl.BlockSpec((b, tl, LANES), lambda h, i: (0, n_l - 1 - i, h))],
        out_shape=[out_sd, out_sd],
        scratch_shapes=[pltpu.VMEM((2 * b, LANES, LANES), F32)],
        compiler_params=_params("parallel", "arbitrary"),
        name="gdn_scan",
    )(qkv, qkv, qkv, gb, qkv, qkv, qkv, gb)


def _gdn_out_kernel(of_ref, ob_ref, gt_ref, w_ref, o_ref):
    o = of_ref[...] + ob_ref[...]
    ms = jnp.mean(o * o, axis=-1, keepdims=True)
    y = o * lax.rsqrt(ms + NORM_EPS) * w_ref[...]
    o_ref[...] = (y * _silu(gt_ref[...])).astype(o_ref.dtype)


def _gdn_out(o_f, o_b, proj, gate_block0, o_norm):
    b, l, d = o_f.shape
    tm = _pick(l, 1024)
    spec = pl.BlockSpec((None, tm, LANES), lambda bi, i, h: (bi, i, h))
    return pl.pallas_call(
        _gdn_out_kernel,
        grid=(b, l // tm, d // LANES),
        in_specs=[spec, spec,
                  pl.BlockSpec((None, tm, LANES), lambda bi, i, h: (bi, i, gate_block0 + h)),
                  pl.BlockSpec((1, LANES), lambda bi, i, h: (0, 0))],
        out_specs=spec,
        out_shape=jax.ShapeDtypeStruct((b, l, d), BF16),
        compiler_params=_params("parallel", "parallel", "parallel"),
        name="gdn_out",
    )(o_f, o_b, proj, o_norm.reshape(1, LANES))


def _gdn_mixer(x, g, sc, sh, gate, params):
    w_in, conv, a_log, dt_bias, o_norm, w_out = params
    d = x.shape[-1]
    n_heads = a_log.shape[1]
    assert d // n_heads == LANES and 4 * n_heads == LANES
    proj = _norm_matmul(x, g, sc, sh, w_in)
    qkv = _conv3(proj, conv, 3 * d, gdn_d=d)
    gb = _gdn_gates(proj, (4 * d) // LANES, a_log, dt_bias)
    o_f, o_b = _gdn_scan(qkv, gb, d)
    y = _gdn_out(o_f, o_b, proj, (3 * d) // LANES, o_norm)
    return _resid_matmul(y, w_out, x, gate)


def _top_rows(s, k, payload=None):
    r_n = s.shape[0]
    row = lax.broadcasted_iota(I32, s.shape, 0)
    vals, idxs, pays = [], [], []
    for _ in range(k):
        m = jnp.max(s, axis=0, keepdims=True)
        idx = jnp.min(jnp.where(s == m, row, r_n), axis=0, keepdims=True)
        hit = row == idx
        vals.append(m)
        idxs.append(idx)
        if payload is not None:
            pays.append(jnp.max(jnp.where(hit, payload, -1), axis=0, keepdims=True))
        s = jnp.where(hit, -jnp.inf, s)
    out = (jnp.concatenate(vals, axis=0), jnp.concatenate(idxs, axis=0))
    if payload is not None:
        out += (jnp.concatenate(pays, axis=0),)
    return out


def _peer_route_kernel(q_ref, khi_ref, klo_ref, ids_ref, gates_ref, *, n_heads, n_keys, key_dim):
    k_top = PEER_TOPK
    for hd in range(n_heads):
        tops = []
        for p in range(2):
            c0 = (hd * 2 + p) * key_dim
            q_hi, q_lo = _split2(q_ref[:, c0:c0 + key_dim])
            s = _dot3(khi_ref[p], klo_ref[p], q_hi, q_lo, _dot_nt)
            tops.append(_top_rows(s, k_top))
        (s0, i0), (s1, i1) = tops
        cand = jnp.concatenate([s0[a:a + 1, :] + s1 for a in range(k_top)], axis=0)
        cid = jnp.concatenate([i0[a:a + 1, :] * n_keys + i1 for a in range(k_top)], axis=0)
        best, _, ids = _top_rows(cand, k_top, payload=cid)
        e = jnp.exp(best - best[0:1, :])
        gates = e / jnp.sum(e, axis=0, keepdims=True)
        ids_ref[hd * k_top:(hd + 1) * k_top, :] = ids
        gates_ref[hd * k_top:(hd + 1) * k_top, :] = gates


def _peer_route(q, keys):
    t_n, qd = q.shape
    _, n_keys, key_dim = keys.shape
    n_heads = qd // (2 * key_dim)
    tm = _pick(t_n, 256, LANES)
    k_hi, k_lo = _split2(keys)
    kern = functools.partial(_peer_route_kernel, n_heads=n_heads, n_keys=n_keys, key_dim=key_dim)
    rows = n_heads * PEER_TOPK
    k_spec = pl.BlockSpec((2, n_keys, key_dim), lambda i: (0, 0, 0))
    o_spec = pl.BlockSpec((rows, tm), lambda i: (0, i))
    return pl.pallas_call(
        kern,
        grid=(t_n // tm,),
        in_specs=[pl.BlockSpec((tm, qd), lambda i: (i, 0)), k_spec, k_spec],
        out_specs=[o_spec, o_spec],
        out_shape=[jax.ShapeDtypeStruct((rows, t_n), I32), jax.ShapeDtypeStruct((rows, t_n), F32)],
        compiler_params=_params("parallel"),
        name="peer_route",
    )(q, k_hi, k_lo)


def _peer_gather_kernel(ids_ref, gates_ref, x_ref, g_ref, sc_ref, sh_ref, og_ref, uv_hbm,
                        o_ref, h_ref, buf, sem, *, tb, n_sel, n_slot, rows, grp):
    x = x_ref[...]
    ms = jnp.sum(jnp.sum(x * x, axis=2, keepdims=True), axis=1, keepdims=True) * (1.0 / (rows * LANES))
    h_ref[...] = (x * lax.rsqrt(ms + NORM_EPS) * g_ref[...]) * (1.0 + sc_ref[...]) + sh_ref[...]

    n_q = rows // grp
    jc_n = SUBLANES
    sel = (lax.broadcasted_iota(I32, (n_sel, n_sel * grp), 1) // grp
           == lax.broadcasted_iota(I32, (n_sel, n_sel * grp), 0)).astype(BF16)
    sel_t = (lax.broadcasted_iota(I32, (n_sel * grp, n_sel), 0) // grp
             == lax.broadcasted_iota(I32, (n_sel * grp, n_sel), 1)).astype(BF16)
    lane_t = lax.broadcasted_iota(I32, (n_sel, tb), 1)

    n_chunk = n_sel // jc_n

    def copies(t, slot):
        def start(j):
            pltpu.make_async_copy(uv_hbm.at[ids_ref[t, j]], buf.at[slot, j], sem.at[slot]).start()
        return [functools.partial(start, j) for j in range(n_sel)]

    def wait(slot):
        pltpu.make_async_copy(uv_hbm.at[pl.ds(0, n_sel)], buf.at[slot], sem.at[slot]).wait()

    def scores(t, slot, tick):
        h = h_ref[t]
        h_q = [h[qi * grp:(qi + 1) * grp] for qi in range(n_q)]
        parts = []
        for jc in range(n_chunk):
            ub = buf[slot, jc * jc_n:(jc + 1) * jc_n, 0:rows, :].astype(F32)
            q = ub[:, 0:grp] * h_q[0]
            for qi in range(1, n_q):
                q = q + ub[:, qi * grp:(qi + 1) * grp] * h_q[qi]
            parts.append(q)
            tick()
        q2 = jnp.concatenate(parts, axis=0).reshape(n_sel * grp, LANES)
        part = _dot(sel, q2.astype(BF16))
        act = jnp.sum(part, axis=1, keepdims=True)
        gate_col = jnp.sum(jnp.where(lane_t == t, gates_ref[...], 0.0), axis=1, keepdims=True)
        wgt = gate_col * (0.5 * act * (1.0 + lax.erf(act * (2.0 ** -0.5))))
        w_b = jnp.broadcast_to(wgt, (n_sel, LANES)).astype(BF16)
        return _dot(sel_t, w_b).reshape(n_sel, grp, LANES)

    def combine(t, slot, w_rep, tick):
        acc = [None] * n_q
        for jc in range(n_chunk):
            vb = buf[slot, jc * jc_n:(jc + 1) * jc_n, rows:2 * rows, :].astype(F32)
            ws = w_rep[jc * jc_n:(jc + 1) * jc_n]
            for qi in range(n_q):
                term = jnp.sum(vb[:, qi * grp:(qi + 1) * grp] * ws, axis=0)
                acc[qi] = term if acc[qi] is None else acc[qi] + term
            tick()
        y = jnp.concatenate(acc, axis=0) if n_q > 1 else acc[0]
        return x_ref[t] + og_ref[...] * y

    def pair(p, slot_pair, prefetch):
        ta, tb_ = 2 * p, 2 * p + 1
        sa, sb = 2 * slot_pair, 2 * slot_pair + 1
        pending = []
        if prefetch:
            nxt_pair = (slot_pair + n_pair - 1) % n_pair
            tn = 2 * (p + n_pair - 1)
            pending = copies(tn, 2 * nxt_pair) + copies(tn + 1, 2 * nxt_pair + 1)
        per_tick = -(-len(pending) // (4 * n_chunk))

        def tick():
            for _ in range(min(per_tick, len(pending))):
                pending.pop(0)()

        wait(sa)
        wait(sb)
        w_a = scores(ta, sa, tick)
        w_b = scores(tb_, sb, tick)
        out_a = combine(ta, sa, w_a, tick)
        out_b = combine(tb_, sb, w_b, tick)
        assert not pending
        o_ref[ta] = out_a
        o_ref[tb_] = out_b

    n_pair = n_slot // 2
    n_group = (tb // 2) // n_pair
    for sp in range(n_pair - 1):
        for start in copies(2 * sp, 2 * sp) + copies(2 * sp + 1, 2 * sp + 1):
            start()

    def group(gi, carry):
        for sp in range(n_pair):
            pair(gi * n_pair + sp, sp, True)
        return carry

    lax.fori_loop(0, n_group - 1, group, 0)
    for sp in range(n_pair):
        pair((n_group - 1) * n_pair + sp, sp, sp == 0)


def _peer_gather(ids, gates_t, x, g, sc, sh, gate, u_tab, v_tab):
    b, l, d = x.shape
    t_n, n_sel = ids.shape
    rows = d // LANES
    grp = min(SUBLANES, rows)
    tb = _pick(l, PEER_TOKENS_PER_STEP, LANES)
    assert tb % PEER_SLOTS == 0 and PEER_SLOTS >= 4 and n_sel % SUBLANES == 0
    per_b = l // tb
    tiles = lambda a: a.reshape(a.shape[0], rows, LANES)
    uv = jnp.concatenate([tiles(u_tab.astype(BF16)), tiles(v_tab.astype(BF16))], axis=1)
    mod_spec = pl.BlockSpec((None, rows, LANES), lambda i: (i // per_b, 0, 0))
    kern = functools.partial(_peer_gather_kernel, tb=tb, n_sel=n_sel, n_slot=PEER_SLOTS, rows=rows, grp=grp)
    out = pl.pallas_call(
        kern,
        grid=(t_n // tb,),
        in_specs=[pl.BlockSpec((tb, n_sel), lambda i: (i, 0), memory_space=pltpu.SMEM),
                  pl.BlockSpec((n_sel, tb), lambda i: (0, i)),
                  pl.BlockSpec((tb, rows, LANES), lambda i: (i, 0, 0)),
                  pl.BlockSpec((rows, LANES), lambda i: (0, 0)),
                  mod_spec, mod_spec, mod_spec,
                  pl.BlockSpec(memory_space=pl.ANY)],
        out_specs=pl.BlockSpec((tb, rows, LANES), lambda i: (i, 0, 0)),
        out_shape=jax.ShapeDtypeStruct((t_n, rows, LANES), F32),
        scratch_shapes=[pltpu.VMEM((tb, rows, LANES), F32),
                        pltpu.VMEM((PEER_SLOTS, n_sel, 2 * rows, LANES), BF16),
                        pltpu.SemaphoreType.DMA((PEER_SLOTS,))],
        compiler_params=_params("arbitrary"),
        name="peer_gather",
    )(ids, gates_t, x.reshape(t_n, rows, LANES), g.reshape(rows, LANES), tiles(sc), tiles(sh), tiles(gate), uv)
    return out.reshape(b, l, d)


def _peer(x, g, sc, sh, gate, params):
    w_q, keys, u_tab, v_tab = params
    b, l, d = x.shape
    q = _norm_matmul(x, g, sc, sh, w_q, three_pass=True)
    ids_t, gates_t = _peer_route(q.reshape(b * l, -1), keys)
    return _peer_gather(ids_t.T, gates_t, x, g, sc, sh, gate, u_tab, v_tab)


def _final_norm_kernel(x_ref, g_ref, o_ref):
    x = x_ref[...]
    ms = jnp.mean(x * x, axis=-1, keepdims=True)
    o_ref[...] = x * lax.rsqrt(ms + NORM_EPS) * g_ref[...]


def _final_norm(x, g):
    b, l, d = x.shape
    tm = _pick(l, 256)
    return pl.pallas_call(
        _final_norm_kernel,
        grid=(b, l // tm),
        in_specs=[pl.BlockSpec((None, tm, d), lambda bi, i: (bi, i, 0)),
                  pl.BlockSpec((1, d), lambda bi, i: (0, 0))],
        out_specs=pl.BlockSpec((None, tm, d), lambda bi, i: (bi, i, 0)),
        out_shape=jax.ShapeDtypeStruct((b, l, d), F32),
        compiler_params=_params("parallel", "parallel"),
        name="final_norm",
    )(x, g.reshape(1, d))


_MIXERS = (_hyena_mixer, _pool_mixer, _gdn_mixer)


def _trunk(x, mods, layers, final_norm):
    d = x.shape[-1]
    for li, (norm_tok, norm_ch, mixer_params, peer_params) in enumerate(layers):
        mod = mods[li].reshape(x.shape[0], 1, N_MOD, d)
        sh_t, sc_t, g_t, sh_c, sc_c, g_c = (mod[:, :, m] for m in range(N_MOD))
        x = _MIXERS[li % len(_MIXERS)](x, norm_tok, sc_t, sh_t, g_t, mixer_params)
        x = _peer(x, norm_ch, sc_c, sh_c, g_c, peer_params)
    return _final_norm(x, final_norm)


def kernel(x_prompt, x_sample, c_prompt, c_sample, l0_ada_w, l0_ada_b, l0_norm_tok, l0_norm_ch, l0_hy_w_in, l0_hy_conv, l0_hy_ffn_w1, l0_hy_ffn_b1, l0_hy_ffn_w2, l0_hy_ffn_b2, l0_hy_ffn_w3, l0_hy_ffn_b3, l0_hy_sin_freq, l0_hy_log_decay, l0_hy_bias, l0_hy_w_out, l0_peer_w_q, l0_peer_keys, l0_peer_u, l0_peer_v, l1_ada_w, l1_ada_b, l1_norm_tok, l1_norm_ch, l1_pool_w_in, l1_pool_w_grp, l1_pool_scale, l1_pool_w_out, l1_peer_w_q, l1_peer_keys, l1_peer_u, l1_peer_v, l2_ada_w, l2_ada_b, l2_norm_tok, l2_norm_ch, l2_gdn_w_in, l2_gdn_conv, l2_gdn_A_log, l2_gdn_dt_bias, l2_gdn_o_norm, l2_gdn_w_out, l2_peer_w_q, l2_peer_keys, l2_peer_u, l2_peer_v, l3_ada_w, l3_ada_b, l3_norm_tok, l3_norm_ch, l3_hy_w_in, l3_hy_conv, l3_hy_ffn_w1, l3_hy_ffn_b1, l3_hy_ffn_w2, l3_hy_ffn_b2, l3_hy_ffn_w3, l3_hy_ffn_b3, l3_hy_sin_freq, l3_hy_log_decay, l3_hy_bias, l3_hy_w_out, l3_peer_w_q, l3_peer_keys, l3_peer_u, l3_peer_v, final_norm):
    layers = (
        (l0_norm_tok, l0_norm_ch,
         (l0_hy_w_in, l0_hy_conv, l0_hy_ffn_w1, l0_hy_ffn_b1, l0_hy_ffn_w2, l0_hy_ffn_b2,
          l0_hy_ffn_w3, l0_hy_ffn_b3, l0_hy_sin_freq, l0_hy_log_decay, l0_hy_bias, l0_hy_w_out),
         (l0_peer_w_q, l0_peer_keys, l0_peer_u, l0_peer_v)),
        (l1_norm_tok, l1_norm_ch,
         (l1_pool_w_in, l1_pool_w_grp, l1_pool_scale, l1_pool_w_out),
         (l1_peer_w_q, l1_peer_keys, l1_peer_u, l1_peer_v)),
        (l2_norm_tok, l2_norm_ch,
         (l2_gdn_w_in, l2_gdn_conv, l2_gdn_A_log, l2_gdn_dt_bias, l2_gdn_o_norm, l2_gdn_w_out),
         (l2_peer_w_q, l2_peer_keys, l2_peer_u, l2_peer_v)),
        (l3_norm_tok, l3_norm_ch,
         (l3_hy_w_in, l3_hy_conv, l3_hy_ffn_w1, l3_hy_ffn_b1, l3_hy_ffn_w2, l3_hy_ffn_b2,
          l3_hy_ffn_w3, l3_hy_ffn_b3, l3_hy_sin_freq, l3_hy_log_decay, l3_hy_bias, l3_hy_w_out),
         (l3_peer_w_q, l3_peer_keys, l3_peer_u, l3_peer_v)),
    )
    ada = ((l0_ada_w, l0_ada_b), (l1_ada_w, l1_ada_b), (l2_ada_w, l2_ada_b), (l3_ada_w, l3_ada_b))
    n_p = c_prompt.shape[0]
    n_s = c_sample.shape[0]
    pad = (-(n_p + n_s)) % SUBLANES
    c_rows = jnp.concatenate([c_prompt, c_sample, jnp.zeros((pad, c_prompt.shape[1]), F32)], axis=0)
    mods = [_ada(c_rows, w, b) for w, b in ada]
    y_prompt = _trunk(x_prompt, [m[:n_p] for m in mods], layers, final_norm)
    y_sample = _trunk(x_sample, [m[n_p:n_p + n_s] for m in mods], layers, final_norm)
    return (y_prompt, y_sample)
```

```python
import functools
import math

import jax
import jax.numpy as jnp
from jax import lax
from jax.experimental import pallas as pl
from jax.experimental.pallas import tpu as pltpu

F32 = jnp.float32
BF16 = jnp.bfloat16
I32 = jnp.int32
HIGHEST = lax.Precision.HIGHEST

NORM_EPS = 1e-6
N_MOD = 6
LANES = 128
SUBLANES = 8
VMEM_LIMIT_BYTES = 56 * 1024 * 1024
FFT_N2 = 128
POOL_WINDOWS = (2, 4, 8, 16)
POOL_HALO = 8
GDN_CHUNK = 64
GDN_HEADS_PER_STEP = 2
PEER_TOPK = 16
PEER_TOKENS_PER_STEP = 128
PEER_SLOTS = 8


def _pick(n, pref, mult=SUBLANES):
    t = (min(pref, n) // mult) * mult
    while t >= mult:
        if n % t == 0:
            return t
        t -= mult
    return n


def _params(*sem):
    return pltpu.CompilerParams(dimension_semantics=sem, vmem_limit_bytes=VMEM_LIMIT_BYTES)


def _split2(x):
    hi = x.astype(BF16)
    lo = (x - hi.astype(F32)).astype(BF16)
    return hi, lo


def _split3(x):
    hi = x.astype(BF16)
    r = x - hi.astype(F32)
    mid = r.astype(BF16)
    lo = (r - mid.astype(F32)).astype(BF16)
    return hi, mid, lo


def _dot(a, b):
    return jnp.dot(a, b, preferred_element_type=F32)


def _dot_nt(a, b):
    return lax.dot_general(a, b, (((1,), (1,)), ((), ())), preferred_element_type=F32)


def _dot_tn(a, b):
    return lax.dot_general(a, b, (((0,), (0,)), ((), ())), preferred_element_type=F32)


def _dot3(a_hi, a_lo, b_hi, b_lo, dot=_dot):
    return dot(a_hi, b_hi) + dot(a_lo, b_hi) + dot(a_hi, b_lo)


def _dot_x3(a, b, dot=_dot):
    a_hi, a_lo = _split2(a)
    b_hi, b_lo = _split2(b)
    return _dot3(a_hi, a_lo, b_hi, b_lo, dot)


def _dot_exact_lhs(a_exact, b):
    a = a_exact.astype(BF16)
    b_hi, b_mid, b_lo = _split3(b)
    return _dot(a, b_hi) + _dot(a, b_mid) + _dot(a, b_lo)


def _silu(x):
    return x * jax.nn.sigmoid(x)


def _ada_kernel(c_ref, w_ref, b_ref, o_ref):
    a = _silu(c_ref[...])
    o_ref[...] = jnp.dot(a, w_ref[...], precision=HIGHEST, preferred_element_type=F32) + b_ref[...]


def _ada(c_rows, w, b):
    rows, d = c_rows.shape
    n = w.shape[1]
    tn = _pick(n, 512, LANES)
    return pl.pallas_call(
        _ada_kernel,
        grid=(n // tn,),
        in_specs=[pl.BlockSpec((rows, d), lambda j: (0, 0)),
                  pl.BlockSpec((d, tn), lambda j: (0, j)),
                  pl.BlockSpec((1, tn), lambda j: (0, j))],
        out_specs=pl.BlockSpec((rows, tn), lambda j: (0, j)),
        out_shape=jax.ShapeDtypeStruct((rows, n), F32),
        compiler_params=_params("parallel"),
        name="ada_mod",
    )(c_rows, w, b.reshape(1, n))


def _modulated_norm(x, g, sc, sh):
    ms = jnp.mean(x * x, axis=-1, keepdims=True)
    return (x * lax.rsqrt(ms + NORM_EPS) * g) * (1.0 + sc) + sh


def _nm_kernel(x_ref, g_ref, sc_ref, sh_ref, w_ref, o_ref, h_ref):
    @pl.when(pl.program_id(2) == 0)
    def _():
        h_ref[...] = _modulated_norm(x_ref[...], g_ref[...], sc_ref[...], sh_ref[...]).astype(BF16)

    o_ref[...] = _dot(h_ref[...], w_ref[...]).astype(o_ref.dtype)


def _nm3_kernel(x_ref, g_ref, sc_ref, sh_ref, whi_ref, wlo_ref, o_ref, hhi_ref, hlo_ref):
    @pl.when(pl.program_id(2) == 0)
    def _():
        hi, lo = _split2(_modulated_norm(x_ref[...], g_ref[...], sc_ref[...], sh_ref[...]))
        hhi_ref[...] = hi
        hlo_ref[...] = lo

    o_ref[...] = _dot3(hhi_ref[...], hlo_ref[...], whi_ref[...], wlo_ref[...]).astype(o_ref.dtype)


def _norm_matmul(x, g, sc, sh, w, three_pass=False, out_dtype=F32):
    b, l, d = x.shape
    n = w.shape[1]
    tm = _pick(l, 256 if three_pass else 512)
    tn = _pick(n, 512, LANES)
    x_spec = pl.BlockSpec((None, tm, d), lambda bi, i, j: (bi, i, 0))
    g_spec = pl.BlockSpec((1, d), lambda bi, i, j: (0, 0))
    m_spec = pl.BlockSpec((None, 1, d), lambda bi, i, j: (bi, 0, 0))
    w_spec = pl.BlockSpec((d, tn), lambda bi, i, j: (0, j))
    o_spec = pl.BlockSpec((None, tm, tn), lambda bi, i, j: (bi, i, j))
    if three_pass:
        w_hi, w_lo = _split2(w)
        kern, w_args, w_specs = _nm3_kernel, (w_hi, w_lo), [w_spec, w_spec]
        scratch = [pltpu.VMEM((tm, d), BF16), pltpu.VMEM((tm, d), BF16)]
    else:
        kern, w_args, w_specs = _nm_kernel, (w.astype(BF16),), [w_spec]
        scratch = [pltpu.VMEM((tm, d), BF16)]
    return pl.pallas_call(
        kern,
        grid=(b, l // tm, n // tn),
        in_specs=[x_spec, g_spec, m_spec, m_spec] + w_specs,
        out_specs=o_spec,
        out_shape=jax.ShapeDtypeStruct((b, l, n), out_dtype),
        scratch_shapes=scratch,
        compiler_params=_params("parallel", "parallel", "arbitrary"),
        name="norm_matmul3" if three_pass else "norm_matmul",
    )(x, g.reshape(1, d), sc, sh, *w_args)


def _resid_mm_kernel(z_ref, w_ref, x_ref, gt_ref, o_ref):
    o_ref[...] = x_ref[...] + gt_ref[...] * _dot(z_ref[...].astype(BF16), w_ref[...])


def _resid_matmul(z, w, x, gate):
    b, l, k = z.shape
    n = w.shape[1]
    tm = _pick(l, 512)
    tn = _pick(n, 512, LANES)
    return pl.pallas_call(
        _resid_mm_kernel,
        grid=(b, l // tm, n // tn),
        in_specs=[pl.BlockSpec((None, tm, k), lambda bi, i, j: (bi, i, 0)),
                  pl.BlockSpec((k, tn), lambda bi, i, j: (0, j)),
                  pl.BlockSpec((None, tm, tn), lambda bi, i, j: (bi, i, j)),
                  pl.BlockSpec((None, 1, tn), lambda bi, i, j: (bi, 0, j))],
        out_specs=pl.BlockSpec((None, tm, tn), lambda bi, i, j: (bi, i, j)),
        out_shape=jax.ShapeDtypeStruct((b, l, n), F32),
        compiler_params=_params("parallel", "parallel", "parallel"),
        name="resid_matmul",
    )(z, w.astype(BF16), x, gate)


def _conv3_kernel(x_ref, p_ref, n_ref, w_ref, o_ref, *, tm, gdn_d, head_scale):
    i = pl.program_id(1)
    last = pl.num_programs(1) - 1
    x = x_ref[...]
    rows = lax.broadcasted_iota(I32, (tm, 1), 0)
    prev_row = jnp.where(i > 0, p_ref[SUBLANES - 1:SUBLANES, :], 0.0)
    next_row = jnp.where(i < last, n_ref[0:1, :], 0.0)
    x_m = jnp.where(rows == 0, prev_row, pltpu.roll(x, 1, axis=0))
    x_p = jnp.where(rows == tm - 1, next_row, pltpu.roll(x, tm - 1, axis=0))
    w = w_ref[...]
    y = x_m * w[0:1, :] + x * w[1:2, :] + x_p * w[2:3, :]
    if not gdn_d:
        o_ref[...] = y
        return
    y = _silu(y)
    sec = (pl.program_id(2) * y.shape[1]) // gdn_d
    for hd in range(y.shape[1] // LANES):
        y_h = y[:, hd * LANES:(hd + 1) * LANES]
        nrm = y_h * lax.rsqrt(jnp.sum(y_h * y_h, axis=-1, keepdims=True) + NORM_EPS)
        nrm = nrm * jnp.where(sec == 0, head_scale, 1.0)
        o_ref[:, hd * LANES:(hd + 1) * LANES] = jnp.where(sec < 2, nrm, y_h)


def _conv3(x, w, n_cols, gdn_d=0):
    b, l, _ = x.shape
    tm = _pick(l, 512)
    tc = _pick(gdn_d if gdn_d else n_cols, 1024, LANES)
    r8 = tm // SUBLANES
    kern = functools.partial(_conv3_kernel, tm=tm, gdn_d=gdn_d, head_scale=float(LANES) ** -0.5)
    return pl.pallas_call(
        kern,
        grid=(b, l // tm, n_cols // tc),
        in_specs=[pl.BlockSpec((None, tm, tc), lambda bi, i, c: (bi, i, c)),
                  pl.BlockSpec((None, SUBLANES, tc), lambda bi, i, c: (bi, jnp.maximum(i * r8 - 1, 0), c)),
                  pl.BlockSpec((None, SUBLANES, tc),
                               lambda bi, i, c: (bi, jnp.minimum((i + 1) * r8, l // SUBLANES - 1), c)),
                  pl.BlockSpec((3, tc), lambda bi, i, c: (0, c))],
        out_specs=pl.BlockSpec((None, tm, tc), lambda bi, i, c: (bi, i, c)),
        out_shape=jax.ShapeDtypeStruct((b, l, n_cols), F32),
        compiler_params=_params("parallel", "parallel", "parallel"),
        name="conv3_gdn" if gdn_d else "conv3",
    )(x, x, x, w)


def _hyfilt_kernel(band_ref, w1_ref, b1_ref, w2_ref, b2_ref, sf_ref, w3_ref, b3_ref, ld_ref,
                   o_ref, ss_ref, hid_ref, *, tl, seq, n_band):
    i = pl.program_id(1)
    r = i * tl + lax.broadcasted_iota(I32, (tl, 1), 0)
    pos = jnp.where(r < seq, r, 2 * seq - r).astype(F32)
    t = pos / float(seq - 1)

    @pl.when(pl.program_id(0) == 0)
    def _():
        omega = (2.0 * math.pi) * pos / float(seq)
        ang = omega * band_ref[...]
        lane = lax.broadcasted_iota(I32, (tl, LANES), 1)
        feats = jnp.where(lane == 0, t,
                          jnp.where(lane <= n_band, jnp.cos(ang),
                                    jnp.where(lane <= 2 * n_band, -jnp.sin(ang), 0.0)))
        sf = sf_ref[...]
        h1 = jnp.sin(sf[0:1, :] * (jnp.dot(feats, w1_ref[...], precision=HIGHEST) + b1_ref[...]))
        hid_ref[i] = jnp.sin(sf[1:2, :] * (jnp.dot(h1, w2_ref[...], precision=HIGHEST) + b2_ref[...]))

    h = jnp.dot(hid_ref[i], w3_ref[...], precision=HIGHEST) + b3_ref[...]
    out = jnp.where(r == seq, 0.0, h * jnp.exp(-t * jnp.exp(ld_ref[...])))
    o_ref[...] = out

    @pl.when(i == 0)
    def _():
        ss_ref[...] = jnp.zeros_like(ss_ref)

    ss_ref[...] += jnp.sum(out * out, axis=0, keepdims=True)


def _hyena_filters(seq, w1, b1, w2, b2, w3, b3, sin_freq, log_decay):
    emb, hid = w1.shape
    n_band = (emb - 1) // 2
    n_order, n_dir, d_model = log_decay.shape
    assert n_dir == 2
    n_out = n_order * d_model
    bands = jnp.linspace(1e-4, n_band - 1, n_band, dtype=F32)
    band_row = jnp.zeros((1, LANES), F32).at[0, 1:1 + n_band].set(bands).at[0, 1 + n_band:1 + 2 * n_band].set(bands)
    w1p = jnp.zeros((LANES, hid), F32).at[:emb].set(w1)
    tl = _pick(seq, 512)
    tn = _pick(d_model, 1024, LANES)
    per_d = d_model // tn
    kern = functools.partial(_hyfilt_kernel, tl=tl, seq=seq, n_band=n_band)
    full = lambda shape: pl.BlockSpec(shape, lambda j, i: (0,) * len(shape))
    src_col = lambda j, i: (0, ((j // per_d) * 2 + (i * tl) // seq) * per_d + j % per_d)
    return pl.pallas_call(
        kern,
        grid=(n_out // tn, 2 * seq // tl),
        in_specs=[full((1, LANES)), full((LANES, hid)), full((1, hid)), full((hid, hid)), full((1, hid)),
                  full((2, hid)),
                  pl.BlockSpec((hid, tn), src_col),
                  pl.BlockSpec((1, tn), src_col),
                  pl.BlockSpec((1, tn), src_col)],
        out_specs=[pl.BlockSpec((tl, tn), lambda j, i: (i, j)),
                   pl.BlockSpec((1, tn), lambda j, i: (0, j))],
        out_shape=[jax.ShapeDtypeStruct((2 * seq, n_out), F32), jax.ShapeDtypeStruct((1, n_out), F32)],
        scratch_shapes=[pltpu.VMEM((2 * seq // tl, tl, hid), F32)],
        compiler_params=_params("arbitrary", "arbitrary"),
        name="hyena_filters",
    )(band_row, w1p, b1.reshape(1, hid), w2, b2.reshape(1, hid), sin_freq, w3, b3.reshape(1, -1),
      log_decay.reshape(1, -1))


def _lmm_kernel(m_ref, x_ref, o_ref, *, gb):
    for g in range(gb):
        o_ref[g] = _dot(m_ref[g], x_ref[g].astype(BF16))


def _lmm_epi_kernel(m_ref, x_ref, z_ref, gt_ref, bias_ref, o_ref, *, gb):
    for g in range(gb):
        conv = _dot(m_ref[g], x_ref[g].astype(BF16))
        o_ref[g] = gt_ref[g] * (conv + z_ref[g] * bias_ref[...])


def _fft_stage1(mats, x, epilogue=None):
    g_n, r_out, r_in = mats.shape
    d = x.shape[-1]
    gb = _pick(g_n, 4, 1)
    td = _pick(d, 1024, LANES)
    m_spec = pl.BlockSpec((gb, r_out, r_in), lambda g, j: (g, 0, 0))
    x_spec = pl.BlockSpec((gb, r_in, td), lambda g, j: (g, 0, j))
    o_spec = pl.BlockSpec((gb, r_out, td), lambda g, j: (g, 0, j))
    if epilogue is None:
        kern, extra, extra_specs = functools.partial(_lmm_kernel, gb=gb), (), []
    else:
        kern = functools.partial(_lmm_epi_kernel, gb=gb)
        extra = epilogue
        extra_specs = [o_spec, o_spec, pl.BlockSpec((1, td), lambda g, j: (0, j))]
    return pl.pallas_call(
        kern,
        grid=(g_n // gb, d // td),
        in_specs=[m_spec, x_spec] + extra_specs,
        out_specs=o_spec,
        out_shape=jax.ShapeDtypeStruct((g_n, r_out, d), F32),
        compiler_params=_params("parallel", "parallel"),
        name="fft_stage1" if epilogue is None else "fft_stage1_inv",
    )(mats, x, *extra)


def _s2f_kernel(m_ref, x_ref, o_ref, *, gb):
    for g in range(gb):
        o_ref[g] = _dot(m_ref[...], x_ref[g].astype(BF16))


def _s2c_kernel(m_ref, i_ref, x_ref, f_ref, ss_ref, o_ref, *, gb, half):
    scale = lax.rsqrt(ss_ref[...] + NORM_EPS)
    for g in range(gb):
        spec = _dot(m_ref[...], x_ref[g].astype(BF16))
        f = f_ref[g]
        s_re, s_im = spec[:half], spec[half:]
        f_re, f_im = f[:half], f[half:]
        prod = jnp.concatenate([s_re * f_re - s_im * f_im, s_re * f_im + s_im * f_re], axis=0) * scale
        o_ref[g] = _dot(i_ref[...], prod.astype(BF16))


def _fft_stage2_fwd(mat, x):
    r = mat.shape[0]
    g_n, _, d = x.shape
    gb = _pick(g_n, 4, 1)
    td = _pick(d, 1024, LANES)
    m_spec = pl.BlockSpec((r, r), lambda g, j: (0, 0))
    x_spec = pl.BlockSpec((gb, r, td), lambda g, j: (g, 0, j))
    return pl.pallas_call(
        functools.partial(_s2f_kernel, gb=gb),
        grid=(g_n // gb, d // td),
        in_specs=[m_spec, x_spec],
        out_specs=x_spec,
        out_shape=jax.ShapeDtypeStruct(x.shape, F32),
        compiler_params=_params("parallel", "parallel"),
        name="fft_stage2_fwd",
    )(mat, x)


def _fft_stage2_conv(mat, imat, x, filt_spec, sumsq):
    r = mat.shape[0]
    g_n, _, d = x.shape
    gb = _pick(g_n, 4, 1)
    td = _pick(d, 1024, LANES)
    m_spec = pl.BlockSpec((r, r), lambda g, j: (0, 0))
    x_spec = pl.BlockSpec((gb, r, td), lambda g, j: (g, 0, j))
    return pl.pallas_call(
        functools.partial(_s2c_kernel, gb=gb, half=r // 2),
        grid=(g_n // gb, d // td),
        in_specs=[m_spec, m_spec, x_spec, x_spec, pl.BlockSpec((1, td), lambda g, j: (0, j))],
        out_specs=x_spec,
        out_shape=jax.ShapeDtypeStruct(x.shape, F32),
        compiler_params=_params("parallel", "parallel"),
        name="fft_stage2_conv",
    )(mat, imat, x, filt_spec, sumsq)


def _dft_tables(seq):
    n = 2 * seq
    n2 = FFT_N2
    n1 = n // n2
    n1h = n1 // 2
    k1 = jnp.arange(n1, dtype=I32)
    n2i = jnp.arange(n2, dtype=I32)
    n1i = jnp.arange(n1, dtype=I32)
    pos = n1i[None, :] * n2 + n2i[:, None]
    prod = (k1[None, :, None] * pos[:, None, :]) % n
    ang = prod.astype(F32) * (2.0 * math.pi / n)
    c, s = jnp.cos(ang), jnp.sin(ang)
    ch, sh = c[:, :, :n1h], s[:, :, :n1h]
    fwd_data = jnp.concatenate([jnp.concatenate([ch, sh], axis=2),
                                jnp.concatenate([-sh, ch], axis=2)], axis=1)
    fwd_real = jnp.concatenate([c, -s], axis=1)
    ct, st = jnp.swapaxes(ch, 1, 2), jnp.swapaxes(sh, 1, 2)
    inv_data = jnp.concatenate([jnp.concatenate([ct, -st], axis=2),
                                jnp.concatenate([st, ct], axis=2)], axis=1) * (1.0 / n)
    a2 = ((n2i[:, None] * n2i[None, :]) % n2).astype(F32) * (2.0 * math.pi / n2)
    c2, s2 = jnp.cos(a2), jnp.sin(a2)
    m2 = jnp.concatenate([jnp.concatenate([c2, s2], axis=1), jnp.concatenate([-s2, c2], axis=1)], axis=0)
    m2i = jnp.concatenate([jnp.concatenate([c2, -s2], axis=1), jnp.concatenate([s2, c2], axis=1)], axis=0)
    bf = lambda a: a.astype(BF16)
    return dict(n1=n1, n1h=n1h, fwd_data=bf(fwd_data), fwd_real=bf(fwd_real), inv_data=bf(inv_data),
                m2=bf(m2), m2i=bf(m2i))


def _to_stage2_layout(a, n1):
    n2, r, d = a.shape
    c = r // n1
    return a.reshape(n2, c, n1, d).transpose(2, 1, 0, 3).reshape(n1, c * n2, d)


def _to_stage1_layout(a, n2):
    n1, r, d = a.shape
    c = r // n2
    return a.reshape(n1, c, n2, d).transpose(2, 1, 0, 3).reshape(n2, c * n1, d)


def _hyena_core(u, two_sided, sumsq, bias):
    b, seq, d3 = u.shape
    d = d3 // 3
    n_order = bias.shape[0]
    assert b == 2
    tab = _dft_tables(seq)
    n1, n1h, n2 = tab["n1"], tab["n1h"], FFT_N2
    ut = u.reshape(b, n1h, n2, 3, d).transpose(3, 2, 0, 1, 4).reshape(3, n2, b * n1h, d)
    f_t = two_sided.reshape(n1, n2, n_order, d).transpose(2, 1, 0, 3)
    z_t = ut[0]
    for o in range(n_order):
        f_spec = _fft_stage2_fwd(tab["m2"], _to_stage2_layout(_fft_stage1(tab["fwd_real"], f_t[o]), n1))
        ss = sumsq.reshape(n_order, 1, d)[o]
        a = _to_stage2_layout(_fft_stage1(tab["fwd_data"], z_t), n1)
        c = _to_stage1_layout(_fft_stage2_conv(tab["m2"], tab["m2i"], a, f_spec, ss), n2)
        z_t = _fft_stage1(tab["inv_data"], c, epilogue=(z_t, ut[1 + o], bias[o].reshape(1, d)))
    return z_t.reshape(n2, b, n1h, d).transpose(1, 2, 0, 3).reshape(b, seq, d)


def _hyena_mixer(x, g, sc, sh, gate, params):
    w_in, conv, w1, b1, w2, b2, w3, b3, sin_freq, log_decay, bias, w_out = params
    seq = x.shape[1]
    y = _norm_matmul(x, g, sc, sh, w_in)
    u = _conv3(y, conv, y.shape[-1])
    filt, sumsq = _hyena_filters(seq, w1, b1, w2, b2, w3, b3, sin_freq, log_decay)
    z = _hyena_core(u, filt, sumsq, bias)
    return _resid_matmul(z, w_out, x, gate)


def _pool_kernel(x_ref, p_ref, n_ref, o_ref, *, tm, seq, group):
    i = pl.program_id(1)
    last = pl.num_programs(1) - 1
    x = x_ref[...]
    prev = jnp.where(i > 0, p_ref[...], 0.0)
    nxt = jnp.where(i < last, n_ref[...], 0.0)
    ext = jnp.concatenate([prev, x, nxt], axis=0)
    gi = (pl.program_id(2) * x.shape[1]) // group
    half = jnp.left_shift(1, gi)
    r = lax.broadcasted_iota(I32, (tm, tm + 2 * POOL_HALO), 0)
    c = lax.broadcasted_iota(I32, (tm, tm + 2 * POOL_HALO), 1)
    band = jnp.logical_and(c >= r + POOL_HALO - half, c < r + POOL_HALO + half).astype(F32)
    win = _dot_exact_lhs(band, ext)
    t = i * tm + lax.broadcasted_iota(I32, (tm, 1), 0)
    cnt = (jnp.minimum(t + half, seq) - jnp.maximum(t - half, 0)).astype(F32)
    o_ref[...] = win / cnt - x


def _pool_windows(uf):
    b, l, d = uf.shape
    group = d // len(POOL_WINDOWS)
    tm = _pick(l, 256)
    tc = _pick(group, 512, LANES)
    r8 = tm // SUBLANES
    return pl.pallas_call(
        functools.partial(_pool_kernel, tm=tm, seq=l, group=group),
        grid=(b, l // tm, d // tc),
        in_specs=[pl.BlockSpec((None, tm, tc), lambda bi, i, c: (bi, i, c)),
                  pl.BlockSpec((None, SUBLANES, tc), lambda bi, i, c: (bi, jnp.maximum(i * r8 - 1, 0), c)),
                  pl.BlockSpec((None, SUBLANES, tc),
                               lambda bi, i, c: (bi, jnp.minimum((i + 1) * r8, l // SUBLANES - 1), c))],
        out_specs=pl.BlockSpec((None, tm, tc), lambda bi, i, c: (bi, i, c)),
        out_shape=jax.ShapeDtypeStruct((b, l, d), F32),
        compiler_params=_params("parallel", "parallel", "parallel"),
        name="pool_windows",
    )(uf, uf, uf)


def _group_mm_kernel(p_ref, w_ref, s_ref, o_ref):
    o_ref[...] = (_dot(p_ref[...].astype(BF16), w_ref[...]) * s_ref[...]).astype(o_ref.dtype)


def _group_matmul(p, w_grp, scale):
    b, l, d = p.shape
    n_g, gd, _ = w_grp.shape
    tm = _pick(l, 512)
    tn = _pick(gd, 512, LANES)
    per = gd // tn
    return pl.pallas_call(
        _group_mm_kernel,
        grid=(b, l // tm, n_g, per),
        in_specs=[pl.BlockSpec((None, tm, gd), lambda bi, i, g, j: (bi, i, g)),
                  pl.BlockSpec((None, gd, tn), lambda bi, i, g, j: (g, 0, j)),
                  pl.BlockSpec((1, tn), lambda bi, i, g, j: (0, g * per + j))],
        out_specs=pl.BlockSpec((None, tm, tn), lambda bi, i, g, j: (bi, i, g * per + j)),
        out_shape=jax.ShapeDtypeStruct((b, l, d), BF16),
        compiler_params=_params("parallel", "parallel", "parallel", "parallel"),
        name="pool_group_matmul",
    )(p, w_grp.astype(BF16), scale.reshape(1, d))


def _pool_mixer(x, g, sc, sh, gate, params):
    w_in, w_grp, scale, w_out = params
    uf = _norm_matmul(x, g, sc, sh, w_in)
    y = _group_matmul(_pool_windows(uf), w_grp, scale)
    return _resid_matmul(y, w_out, x, gate)


def _gdn_gates_kernel(x_ref, alog_ref, dtb_ref, o_ref, *, n_heads):
    x = x_ref[...]
    lane = lax.broadcasted_iota(I32, x.shape, 1)
    decay = -jnp.exp(alog_ref[...]) * jax.nn.softplus(x + dtb_ref[...])
    o_ref[...] = jnp.where(lane < 2 * n_heads, jax.nn.sigmoid(x), decay)


def _gdn_gates(proj, col_block, a_log, dt_bias):
    b, l, _ = proj.shape
    n_heads = a_log.shape[1]
    pad = jnp.zeros((1, 2 * n_heads), F32)
    alog_row = jnp.concatenate([pad, a_log.reshape(1, 2 * n_heads)], axis=1)
    dtb_row = jnp.concatenate([pad, dt_bias.reshape(1, 2 * n_heads)], axis=1)
    tm = _pick(l, 1024)
    w = 4 * n_heads
    return pl.pallas_call(
        functools.partial(_gdn_gates_kernel, n_heads=n_heads),
        grid=(b, l // tm),
        in_specs=[pl.BlockSpec((None, tm, w), lambda bi, i: (bi, i, col_block)),
                  pl.BlockSpec((1, w), lambda bi, i: (0, 0)),
                  pl.BlockSpec((1, w), lambda bi, i: (0, 0))],
        out_specs=pl.BlockSpec((None, tm, w), lambda bi, i: (bi, i, 0)),
        out_shape=jax.ShapeDtypeStruct((b, l, w), F32),
        compiler_params=_params("parallel", "parallel"),
        name="gdn_gates",
    )(proj, alog_row, dtb_row)


def _unit_triangular_inverse(a, row, col):
    n = a[0].shape[0]
    eye = (row == col).astype(F32)
    blk = lambda s: (row // s) == (col // s)
    a8 = [jnp.where(blk(8), x, 0.0) for x in a]
    a8_2 = _each(_dot_x3, a8, a8)
    a8_4 = _each(_dot_x3, a8_2, a8_2)
    t = _each(_dot_x3, [eye - x for x in a8], [eye + x for x in a8_2])
    t = _each(_dot_x3, t, [eye + x for x in a8_4])
    s = 8
    while s < n:
        mask = jnp.logical_and(blk(2 * s), jnp.logical_not(blk(s)))
        off = [jnp.where(mask, x, 0.0) for x in a]
        corr = _each(_dot_x3, _each(_dot_x3, t, off), t)
        t = [x - y for x, y in zip(t, corr)]
        s *= 2
    return t


def _each(fn, *lists):
    return [fn(*xs) for xs in zip(*lists)]


def _gdn_chunks(chains, head, row, col, lane):
    c_n = GDN_CHUNK
    q, k, v, gb, s_mat, rev, beta0, g0 = (list(z) for z in zip(*chains))
    incl = [(row <= col) if r else (row >= col) for r in rev]
    strict = [(row < col) if r else (row > col) for r in rev]
    edge = [0 if r else c_n - 1 for r in rev]
    eye = row == col
    ones = jnp.ones((c_n, c_n), F32)
    beta = [jnp.sum(jnp.where(lane == b0 + head, x, 0.0), axis=1, keepdims=True) for x, b0 in zip(gb, beta0)]
    g = [jnp.sum(jnp.where(lane == b0 + head, x, 0.0), axis=1, keepdims=True) for x, b0 in zip(gb, g0)]
    gc = [_dot_exact_lhs(m.astype(F32), jnp.broadcast_to(x, (c_n, LANES))) for m, x in zip(incl, g)]
    gc_col = [x[:, :c_n] for x in gc]
    gc_row = [_dot_exact_lhs(ones, jnp.where(eye, x, 0.0)) for x in gc_col]
    decay = [jnp.where(m, jnp.exp(jnp.where(m, c - r, 0.0)), 0.0) for m, c, r in zip(incl, gc_col, gc_row)]
    kb = [x * b for x, b in zip(k, beta)]
    vb = [x * b for x, b in zip(v, beta)]
    k16 = [x.astype(BF16) for x in k]
    kk = _each(_dot_nt, [x.astype(BF16) for x in kb], k16)
    a_kk = [jnp.where(m, x * d, 0.0) for m, x, d in zip(strict, kk, decay)]
    t_inv = _unit_triangular_inverse(a_kk, row, col)
    e_gc = [jnp.exp(x) for x in gc]
    rhs = [jnp.concatenate([x, y * e], axis=1) for x, y, e in zip(vb, kb, e_gc)]
    sol = _each(_dot_x3, t_inv, rhs)
    qk = _each(_dot_nt, [x.astype(BF16) for x in q], k16)
    a_qk = [(x * d).astype(BF16) for x, d in zip(qk, decay)]
    gc_edge = [x[e:e + 1, :] for x, e in zip(gc, edge)]
    q_s = [(x * e).astype(BF16) for x, e in zip(q, e_gc)]
    k_tail = [(x * jnp.exp(ge - c)).astype(BF16) for x, ge, c in zip(k, gc_edge, gc)]
    s16 = [x.astype(BF16) for x in s_mat]
    ws_s = _each(_dot, [x[:, LANES:].astype(BF16) for x in sol], s16)
    u16 = [(x[:, :LANES] - y).astype(BF16) for x, y in zip(sol, ws_s)]
    o_state = _each(_dot, q_s, s16)
    o_local = _each(_dot, a_qk, u16)
    s_add = _each(_dot_tn, k_tail, u16)
    o = [x + y for x, y in zip(o_state, o_local)]
    s_new = [x * jnp.exp(ge) + y for x, ge, y in zip(s_mat, gc_edge, s_add)]
    return o, s_new


def _gdn_scan_kernel(qf_ref, kf_ref, vf_ref, gf_ref, qb_ref, kb_ref, vb_ref, gb_ref, of_ref, ob_ref, s_ref,
                     *, n_chunks, n_batch, n_heads):
    head = pl.program_id(0) * GDN_HEADS_PER_STEP

    @pl.when(pl.program_id(1) == 0)
    def _():
        s_ref[...] = jnp.zeros_like(s_ref)

    c_n = GDN_CHUNK
    row = lax.broadcasted_iota(I32, (c_n, c_n), 0)
    col = lax.broadcasted_iota(I32, (c_n, c_n), 1)
    lane = lax.broadcasted_iota(I32, (c_n, LANES), 1)
    dirs = ((qf_ref, kf_ref, vf_ref, gf_ref, of_ref, False, 0, 2 * n_heads),
            (qb_ref, kb_ref, vb_ref, gb_ref, ob_ref, True, n_heads, 3 * n_heads))

    def chunk(ci, carry):
        chains, dests = [], []
        for di, (q_ref, k_ref, v_ref, g_ref, o_ref, rev, beta0, g0) in enumerate(dirs):
            cj = (n_chunks - 1 - ci) if rev else ci
            r0 = pl.multiple_of(cj * c_n, c_n)
            for bi in range(n_batch):
                gates = g_ref[bi, pl.ds(r0, c_n), :]
                for hh in range(GDN_HEADS_PER_STEP):
                    si = (di * n_batch + bi) * GDN_HEADS_PER_STEP + hh
                    cols = slice(hh * LANES, (hh + 1) * LANES)
                    chains.append((q_ref[bi, pl.ds(r0, c_n), cols], k_ref[bi, pl.ds(r0, c_n), cols],
                                   v_ref[bi, pl.ds(r0, c_n), cols], gates, s_ref[si],
                                   rev, beta0 + hh, g0 + hh))
                    dests.append((o_ref, bi, r0, cols, si))
        o, s_new = _gdn_chunks(chains, head, row, col, lane)
        for o_c, s_c, (o_ref, bi, r0, cols, si) in zip(o, s_new, dests):
            s_ref[si] = s_c
            o_ref[bi, pl.ds(r0, c_n), cols] = o_c
        return carry

    lax.fori_loop(0, n_chunks, chunk, 0)


def _gdn_scan(qkv, gb, d_model):
    b, l, _ = qkv.shape
    n_heads = d_model // LANES
    tl = _pick(l, 512, GDN_CHUNK)
    n_l = l // tl
    kern = functools.partial(_gdn_scan_kernel, n_chunks=tl // GDN_CHUNK, n_batch=b, n_heads=n_heads)
    hps = GDN_HEADS_PER_STEP
    assert n_heads % hps == 0
    n_hb = n_heads // hps
    wide = hps * LANES
    fwd = lambda sec: pl.BlockSpec((b, tl, wide), lambda h, i: (0, i, sec * n_hb + h))
    bwd = lambda sec: pl.BlockSpec((b, tl, wide), lambda h, i: (0, n_l - 1 - i, sec * n_hb + h))
    gate_f = pl.BlockSpec((b, tl, gb.shape[-1]), lambda h, i: (0, i, 0))
    gate_b = pl.BlockSpec((b, tl, gb.shape[-1]), lambda h, i: (0, n_l - 1 - i, 0))
    out_sd = jax.ShapeDtypeStruct((b, l, d_model), F32)
    return pl.pallas_call(
        kern,
        grid=(n_hb, n_l),
        in_specs=[fwd(0), fwd(1), fwd(2), gate_f, bwd(0), bwd(1), bwd(2), gate_b],
        out_specs=[pl.BlockSpec((b, tl, wide), lambda h, i: (0, i, h)),
                   pl.BlockSpec((b, tl, wide), lambda h, i: (0, n_l - 1 - i, h))],
        out_shape=[out_sd, out_sd],
        scratch_shapes=[pltpu.VMEM((2 * b * hps, LANES, LANES), F32)],
        compiler_params=_params("parallel", "arbitrary"),
        name="gdn_scan",
    )(qkv, qkv, qkv, gb, qkv, qkv, qkv, gb)


def _gdn_out_kernel(of_ref, ob_ref, gt_ref, w_ref, o_ref):
    o = of_ref[...] + ob_ref[...]
    ms = jnp.mean(o * o, axis=-1, keepdims=True)
    y = o * lax.rsqrt(ms + NORM_EPS) * w_ref[...]
    o_ref[...] = (y * _silu(gt_ref[...])).astype(o_ref.dtype)


def _gdn_out(o_f, o_b, proj, gate_block0, o_norm):
    b, l, d = o_f.shape
    tm = _pick(l, 1024)
    spec = pl.BlockSpec((None, tm, LANES), lambda bi, i, h: (bi, i, h))
    return pl.pallas_call(
        _gdn_out_kernel,
        grid=(b, l // tm, d // LANES),
        in_specs=[spec, spec,
                  pl.BlockSpec((None, tm, LANES), lambda bi, i, h: (bi, i, gate_block0 + h)),
                  pl.BlockSpec((1, LANES), lambda bi, i, h: (0, 0))],
        out_specs=spec,
        out_shape=jax.ShapeDtypeStruct((b, l, d), BF16),
        compiler_params=_params("parallel", "parallel", "parallel"),
        name="gdn_out",
    )(o_f, o_b, proj, o_norm.reshape(1, LANES))


def _gdn_mixer(x, g, sc, sh, gate, params):
    w_in, conv, a_log, dt_bias, o_norm, w_out = params
    d = x.shape[-1]
    n_heads = a_log.shape[1]
    assert d // n_heads == LANES and 4 * n_heads == LANES
    proj = _norm_matmul(x, g, sc, sh, w_in)
    qkv = _conv3(proj, conv, 3 * d, gdn_d=d)
    gb = _gdn_gates(proj, (4 * d) // LANES, a_log, dt_bias)
    o_f, o_b = _gdn_scan(qkv, gb, d)
    y = _gdn_out(o_f, o_b, proj, (3 * d) // LANES, o_norm)
    return _resid_matmul(y, w_out, x, gate)


def _top_rows(s, k, payload=None):
    r_n = s.shape[0]
    row = lax.broadcasted_iota(I32, s.shape, 0)
    vals, idxs, pays = [], [], []
    for _ in range(k):
        m = jnp.max(s, axis=0, keepdims=True)
        idx = jnp.min(jnp.where(s == m, row, r_n), axis=0, keepdims=True)
        hit = row == idx
        vals.append(m)
        idxs.append(idx)
        if payload is not None:
            pays.append(jnp.max(jnp.where(hit, payload, -1), axis=0, keepdims=True))
        s = jnp.where(hit, -jnp.inf, s)
    out = (jnp.concatenate(vals, axis=0), jnp.concatenate(idxs, axis=0))
    if payload is not None:
        out += (jnp.concatenate(pays, axis=0),)
    return out


def _peer_route_kernel(q_ref, khi_ref, klo_ref, ids_ref, gates_ref, *, n_heads, n_keys, key_dim):
    k_top = PEER_TOPK
    for hd in range(n_heads):
        tops = []
        for p in range(2):
            c0 = (hd * 2 + p) * key_dim
            q_hi, q_lo = _split2(q_ref[:, c0:c0 + key_dim])
            s = _dot3(khi_ref[p], klo_ref[p], q_hi, q_lo, _dot_nt)
            tops.append(_top_rows(s, k_top))
        (s0, i0), (s1, i1) = tops
        cand = jnp.concatenate([s0[a:a + 1, :] + s1 for a in range(k_top)], axis=0)
        cid = jnp.concatenate([i0[a:a + 1, :] * n_keys + i1 for a in range(k_top)], axis=0)
        best, _, ids = _top_rows(cand, k_top, payload=cid)
        e = jnp.exp(best - best[0:1, :])
        gates = e / jnp.sum(e, axis=0, keepdims=True)
        ids_ref[hd * k_top:(hd + 1) * k_top, :] = ids
        gates_ref[hd * k_top:(hd + 1) * k_top, :] = gates


def _peer_route(q, keys):
    t_n, qd = q.shape
    _, n_keys, key_dim = keys.shape
    n_heads = qd // (2 * key_dim)
    tm = _pick(t_n, 256, LANES)
    k_hi, k_lo = _split2(keys)
    kern = functools.partial(_peer_route_kernel, n_heads=n_heads, n_keys=n_keys, key_dim=key_dim)
    rows = n_heads * PEER_TOPK
    k_spec = pl.BlockSpec((2, n_keys, key_dim), lambda i: (0, 0, 0))
    o_spec = pl.BlockSpec((rows, tm), lambda i: (0, i))
    return pl.pallas_call(
        kern,
        grid=(t_n // tm,),
        in_specs=[pl.BlockSpec((tm, qd), lambda i: (i, 0)), k_spec, k_spec],
        out_specs=[o_spec, o_spec],
        out_shape=[jax.ShapeDtypeStruct((rows, t_n), I32), jax.ShapeDtypeStruct((rows, t_n), F32)],
        compiler_params=_params("parallel"),
        name="peer_route",
    )(q, k_hi, k_lo)


def _peer_gather_kernel(ids_ref, gates_ref, x_ref, g_ref, sc_ref, sh_ref, og_ref, uv_hbm,
                        o_ref, h_ref, buf, sem, *, tb, n_sel, n_slot, rows, grp):
    x = x_ref[...]
    ms = jnp.sum(jnp.sum(x * x, axis=2, keepdims=True), axis=1, keepdims=True) * (1.0 / (rows * LANES))
    h_ref[...] = (x * lax.rsqrt(ms + NORM_EPS) * g_ref[...]) * (1.0 + sc_ref[...]) + sh_ref[...]

    n_q = rows // grp
    jc_n = SUBLANES
    sel = (lax.broadcasted_iota(I32, (n_sel, n_sel * grp), 1) // grp
           == lax.broadcasted_iota(I32, (n_sel, n_sel * grp), 0)).astype(BF16)
    sel_t = (lax.broadcasted_iota(I32, (n_sel * grp, n_sel), 0) // grp
             == lax.broadcasted_iota(I32, (n_sel * grp, n_sel), 1)).astype(BF16)
    lane_t = lax.broadcasted_iota(I32, (n_sel, tb), 1)

    n_chunk = n_sel // jc_n

    def copies(t, slot):
        def start(j):
            pltpu.make_async_copy(uv_hbm.at[ids_ref[t, j]], buf.at[slot, j], sem.at[slot]).start()
        return [functools.partial(start, j) for j in range(n_sel)]

    def wait(slot):
        pltpu.make_async_copy(uv_hbm.at[pl.ds(0, n_sel)], buf.at[slot], sem.at[slot]).wait()

    def scores(t, slot, tick):
        h = h_ref[t]
        h_q = [h[qi * grp:(qi + 1) * grp] for qi in range(n_q)]
        parts = []
        for jc in range(n_chunk):
            ub = buf[slot, jc * jc_n:(jc + 1) * jc_n, 0:rows, :].astype(F32)
            q = ub[:, 0:grp] * h_q[0]
            for qi in range(1, n_q):
                q = q + ub[:, qi * grp:(qi + 1) * grp] * h_q[qi]
            parts.append(q)
            tick()
        q2 = jnp.concatenate(parts, axis=0).reshape(n_sel * grp, LANES)
        part = _dot(sel, q2.astype(BF16))
        act = jnp.sum(part, axis=1, keepdims=True)
        gate_col = jnp.sum(jnp.where(lane_t == t, gates_ref[...], 0.0), axis=1, keepdims=True)
        wgt = gate_col * (0.5 * act * (1.0 + lax.erf(act * (2.0 ** -0.5))))
        w_b = jnp.broadcast_to(wgt, (n_sel, LANES)).astype(BF16)
        return _dot(sel_t, w_b).reshape(n_sel, grp, LANES)

    def combine(t, slot, w_rep, tick):
        acc = [None] * n_q
        for jc in range(n_chunk):
            vb = buf[slot, jc * jc_n:(jc + 1) * jc_n, rows:2 * rows, :].astype(F32)
            ws = w_rep[jc * jc_n:(jc + 1) * jc_n]
            for qi in range(n_q):
                term = jnp.sum(vb[:, qi * grp:(qi + 1) * grp] * ws, axis=0)
                acc[qi] = term if acc[qi] is None else acc[qi] + term
            tick()
        y = jnp.concatenate(acc, axis=0) if n_q > 1 else acc[0]
        return x_ref[t] + og_ref[...] * y

    def pair(p, slot_pair, prefetch):
        ta, tb_ = 2 * p, 2 * p + 1
        sa, sb = 2 * slot_pair, 2 * slot_pair + 1
        pending = []
        if prefetch:
            nxt_pair = (slot_pair + n_pair - 1) % n_pair
            tn = 2 * (p + n_pair - 1)
            pending = copies(tn, 2 * nxt_pair) + copies(tn + 1, 2 * nxt_pair + 1)
        per_tick = -(-len(pending) // (4 * n_chunk))

        def tick():
            for _ in range(min(per_tick, len(pending))):
                pending.pop(0)()

        wait(sa)
        wait(sb)
        w_a = scores(ta, sa, tick)
        w_b = scores(tb_, sb, tick)
        out_a = combine(ta, sa, w_a, tick)
        out_b = combine(tb_, sb, w_b, tick)
        assert not pending
        o_ref[ta] = out_a
        o_ref[tb_] = out_b

    n_pair = n_slot // 2
    n_group = (tb // 2) // n_pair
    for sp in range(n_pair - 1):
        for start in copies(2 * sp, 2 * sp) + copies(2 * sp + 1, 2 * sp + 1):
            start()

    def group(gi, carry):
        for sp in range(n_pair):
            pair(gi * n_pair + sp, sp, True)
        return carry

    lax.fori_loop(0, n_group - 1, group, 0)
    for sp in range(n_pair):
        pair((n_group - 1) * n_pair + sp, sp, sp == 0)


def _peer_gather(ids, gates_t, x, g, sc, sh, gate, u_tab, v_tab):
    b, l, d = x.shape
    t_n, n_sel = ids.shape
    rows = d // LANES
    grp = min(SUBLANES, rows)
    tb = _pick(l, PEER_TOKENS_PER_STEP, LANES)
    assert tb % PEER_SLOTS == 0 and PEER_SLOTS >= 4 and n_sel % SUBLANES == 0
    per_b = l // tb
    tiles = lambda a: a.reshape(a.shape[0], rows, LANES)
    uv = jnp.concatenate([tiles(u_tab.astype(BF16)), tiles(v_tab.astype(BF16))], axis=1)
    mod_spec = pl.BlockSpec((None, rows, LANES), lambda i: (i // per_b, 0, 0))
    kern = functools.partial(_peer_gather_kernel, tb=tb, n_sel=n_sel, n_slot=PEER_SLOTS, rows=rows, grp=grp)
    out = pl.pallas_call(
        kern,
        grid=(t_n // tb,),
        in_specs=[pl.BlockSpec((tb, n_sel), lambda i: (i, 0), memory_space=pltpu.SMEM),
                  pl.BlockSpec((n_sel, tb), lambda i: (0, i)),
                  pl.BlockSpec((tb, rows, LANES), lambda i: (i, 0, 0)),
                  pl.BlockSpec((rows, LANES), lambda i: (0, 0)),
                  mod_spec, mod_spec, mod_spec,
                  pl.BlockSpec(memory_space=pl.ANY)],
        out_specs=pl.BlockSpec((tb, rows, LANES), lambda i: (i, 0, 0)),
        out_shape=jax.ShapeDtypeStruct((t_n, rows, LANES), F32),
        scratch_shapes=[pltpu.VMEM((tb, rows, LANES), F32),
                        pltpu.VMEM((PEER_SLOTS, n_sel, 2 * rows, LANES), BF16),
                        pltpu.SemaphoreType.DMA((PEER_SLOTS,))],
        compiler_params=_params("arbitrary"),
        name="peer_gather",
    )(ids, gates_t, x.reshape(t_n, rows, LANES), g.reshape(rows, LANES), tiles(sc), tiles(sh), tiles(gate), uv)
    return out.reshape(b, l, d)


def _peer(x, g, sc, sh, gate, params):
    w_q, keys, u_tab, v_tab = params
    b, l, d = x.shape
    q = _norm_matmul(x, g, sc, sh, w_q, three_pass=True)
    ids_t, gates_t = _peer_route(q.reshape(b * l, -1), keys)
    return _peer_gather(ids_t.T, gates_t, x, g, sc, sh, gate, u_tab, v_tab)


def _final_norm_kernel(x_ref, g_ref, o_ref):
    x = x_ref[...]
    ms = jnp.mean(x * x, axis=-1, keepdims=True)
    o_ref[...] = x * lax.rsqrt(ms + NORM_EPS) * g_ref[...]


def _final_norm(x, g):
    b, l, d = x.shape
    tm = _pick(l, 256)
    return pl.pallas_call(
        _final_norm_kernel,
        grid=(b, l // tm),
        in_specs=[pl.BlockSpec((None, tm, d), lambda bi, i: (bi, i, 0)),
                  pl.BlockSpec((1, d), lambda bi, i: (0, 0))],
        out_specs=pl.BlockSpec((None, tm, d), lambda bi, i: (bi, i, 0)),
        out_shape=jax.ShapeDtypeStruct((b, l, d), F32),
        compiler_params=_params("parallel", "parallel"),
        name="final_norm",
    )(x, g.reshape(1, d))


_MIXERS = (_hyena_mixer, _pool_mixer, _gdn_mixer)


def _trunk(x, mods, layers, final_norm):
    d = x.shape[-1]
    for li, (norm_tok, norm_ch, mixer_params, peer_params) in enumerate(layers):
        mod = mods[li].reshape(x.shape[0], 1, N_MOD, d)
        sh_t, sc_t, g_t, sh_c, sc_c, g_c = (mod[:, :, m] for m in range(N_MOD))
        x = _MIXERS[li % len(_MIXERS)](x, norm_tok, sc_t, sh_t, g_t, mixer_params)
        x = _peer(x, norm_ch, sc_c, sh_c, g_c, peer_params)
    return _final_norm(x, final_norm)


def kernel(x_prompt, x_sample, c_prompt, c_sample, l0_ada_w, l0_ada_b, l0_norm_tok, l0_norm_ch, l0_hy_w_in, l0_hy_conv, l0_hy_ffn_w1, l0_hy_ffn_b1, l0_hy_ffn_w2, l0_hy_ffn_b2, l0_hy_ffn_w3, l0_hy_ffn_b3, l0_hy_sin_freq, l0_hy_log_decay, l0_hy_bias, l0_hy_w_out, l0_peer_w_q, l0_peer_keys, l0_peer_u, l0_peer_v, l1_ada_w, l1_ada_b, l1_norm_tok, l1_norm_ch, l1_pool_w_in, l1_pool_w_grp, l1_pool_scale, l1_pool_w_out, l1_peer_w_q, l1_peer_keys, l1_peer_u, l1_peer_v, l2_ada_w, l2_ada_b, l2_norm_tok, l2_norm_ch, l2_gdn_w_in, l2_gdn_conv, l2_gdn_A_log, l2_gdn_dt_bias, l2_gdn_o_norm, l2_gdn_w_out, l2_peer_w_q, l2_peer_keys, l2_peer_u, l2_peer_v, l3_ada_w, l3_ada_b, l3_norm_tok, l3_norm_ch, l3_hy_w_in, l3_hy_conv, l3_hy_ffn_w1, l3_hy_ffn_b1, l3_hy_ffn_w2, l3_hy_ffn_b2, l3_hy_ffn_w3, l3_hy_ffn_b3, l3_hy_sin_freq, l3_hy_log_decay, l3_hy_bias, l3_hy_w_out, l3_peer_w_q, l3_peer_keys, l3_peer_u, l3_peer_v, final_norm):
    layers = (
        (l0_norm_tok, l0_norm_ch,
         (l0_hy_w_in, l0_hy_conv, l0_hy_ffn_w1, l0_hy_ffn_b1, l0_hy_ffn_w2, l0_hy_ffn_b2,
          l0_hy_ffn_w3, l0_hy_ffn_b3, l0_hy_sin_freq, l0_hy_log_decay, l0_hy_bias, l0_hy_w_out),
         (l0_peer_w_q, l0_peer_keys, l0_peer_u, l0_peer_v)),
        (l1_norm_tok, l1_norm_ch,
         (l1_pool_w_in, l1_pool_w_grp, l1_pool_scale, l1_pool_w_out),
         (l1_peer_w_q, l1_peer_keys, l1_peer_u, l1_peer_v)),
        (l2_norm_tok, l2_norm_ch,
         (l2_gdn_w_in, l2_gdn_conv, l2_gdn_A_log, l2_gdn_dt_bias, l2_gdn_o_norm, l2_gdn_w_out),
         (l2_peer_w_q, l2_peer_keys, l2_peer_u, l2_peer_v)),
        (l3_norm_tok, l3_norm_ch,
         (l3_hy_w_in, l3_hy_conv, l3_hy_ffn_w1, l3_hy_ffn_b1, l3_hy_ffn_w2, l3_hy_ffn_b2,
          l3_hy_ffn_w3, l3_hy_ffn_b3, l3_hy_sin_freq, l3_hy_log_decay, l3_hy_bias, l3_hy_w_out),
         (l3_peer_w_q, l3_peer_keys, l3_peer_u, l3_peer_v)),
    )
    ada = ((l0_ada_w, l0_ada_b), (l1_ada_w, l1_ada_b), (l2_ada_w, l2_ada_b), (l3_ada_w, l3_ada_b))
    n_p = c_prompt.shape[0]
    n_s = c_sample.shape[0]
    pad = (-(n_p + n_s)) % SUBLANES
    c_rows = jnp.concatenate([c_prompt, c_sample, jnp.zeros((pad, c_prompt.shape[1]), F32)], axis=0)
    mods = [_ada(c_rows, w, b) for w, b in ada]
    y_prompt = _trunk(x_prompt, [m[:n_p] for m in mods], layers, final_norm)
    y_sample = _trunk(x_sample, [m[n_p:n_p + n_s] for m in mods], layers, final_norm)
    return (y_prompt, y_sample)
```

```python
import functools
import math

import jax
import jax.numpy as jnp
from jax import lax
from jax.experimental import pallas as pl
from jax.experimental.pallas import tpu as pltpu

F32 = jnp.float32
BF16 = jnp.bfloat16
I32 = jnp.int32
HIGHEST = lax.Precision.HIGHEST

NORM_EPS = 1e-6
N_MOD = 6
LANES = 128
SUBLANES = 8
VMEM_LIMIT_BYTES = 56 * 1024 * 1024
FFT_N2 = 128
POOL_WINDOWS = (2, 4, 8, 16)
POOL_HALO = 8
GDN_CHUNK = 64
GDN_HEADS_PER_STEP = 2
PEER_TOPK = 16
PEER_TOKENS_PER_STEP = 128
PEER_SLOTS = 8


def _pick(n, pref, mult=SUBLANES):
    t = (min(pref, n) // mult) * mult
    while t >= mult:
        if n % t == 0:
            return t
        t -= mult
    return n


def _params(*sem):
    return pltpu.CompilerParams(dimension_semantics=sem, vmem_limit_bytes=VMEM_LIMIT_BYTES)


def _split2(x):
    hi = x.astype(BF16)
    lo = (x - hi.astype(F32)).astype(BF16)
    return hi, lo


def _split3(x):
    hi = x.astype(BF16)
    r = x - hi.astype(F32)
    mid = r.astype(BF16)
    lo = (r - mid.astype(F32)).astype(BF16)
    return hi, mid, lo


def _dot(a, b):
    return jnp.dot(a, b, preferred_element_type=F32)


def _dot_nt(a, b):
    return lax.dot_general(a, b, (((1,), (1,)), ((), ())), preferred_element_type=F32)


def _dot_tn(a, b):
    return lax.dot_general(a, b, (((0,), (0,)), ((), ())), preferred_element_type=F32)


def _dot3(a_hi, a_lo, b_hi, b_lo, dot=_dot):
    return dot(a_hi, b_hi) + dot(a_lo, b_hi) + dot(a_hi, b_lo)


def _dot_x3(a, b, dot=_dot):
    a_hi, a_lo = _split2(a)
    b_hi, b_lo = _split2(b)
    return _dot3(a_hi, a_lo, b_hi, b_lo, dot)


def _dot_exact_lhs(a_exact, b):
    a = a_exact.astype(BF16)
    b_hi, b_mid, b_lo = _split3(b)
    return _dot(a, b_hi) + _dot(a, b_mid) + _dot(a, b_lo)


def _silu(x):
    return x * jax.nn.sigmoid(x)


def _ada_kernel(c_ref, w_ref, b_ref, o_ref):
    a = _silu(c_ref[...])
    o_ref[...] = jnp.dot(a, w_ref[...], precision=HIGHEST, preferred_element_type=F32) + b_ref[...]


def _ada(c_rows, w, b):
    rows, d = c_rows.shape
    n = w.shape[1]
    tn = _pick(n, 512, LANES)
    return pl.pallas_call(
        _ada_kernel,
        grid=(n // tn,),
        in_specs=[pl.BlockSpec((rows, d), lambda j: (0, 0)),
                  pl.BlockSpec((d, tn), lambda j: (0, j)),
                  pl.BlockSpec((1, tn), lambda j: (0, j))],
        out_specs=pl.BlockSpec((rows, tn), lambda j: (0, j)),
        out_shape=jax.ShapeDtypeStruct((rows, n), F32),
        compiler_params=_params("parallel"),
        name="ada_mod",
    )(c_rows, w, b.reshape(1, n))


def _modulated_norm(x, g, sc, sh):
    ms = jnp.mean(x * x, axis=-1, keepdims=True)
    return (x * lax.rsqrt(ms + NORM_EPS) * g) * (1.0 + sc) + sh


def _nm_kernel(x_ref, g_ref, sc_ref, sh_ref, w_ref, o_ref, h_ref):
    @pl.when(pl.program_id(2) == 0)
    def _():
        h_ref[...] = _modulated_norm(x_ref[...], g_ref[...], sc_ref[...], sh_ref[...]).astype(BF16)

    o_ref[...] = _dot(h_ref[...], w_ref[...]).astype(o_ref.dtype)


def _nm3_kernel(x_ref, g_ref, sc_ref, sh_ref, whi_ref, wlo_ref, o_ref, hhi_ref, hlo_ref):
    @pl.when(pl.program_id(2) == 0)
    def _():
        hi, lo = _split2(_modulated_norm(x_ref[...], g_ref[...], sc_ref[...], sh_ref[...]))
        hhi_ref[...] = hi
        hlo_ref[...] = lo

    o_ref[...] = _dot3(hhi_ref[...], hlo_ref[...], whi_ref[...], wlo_ref[...]).astype(o_ref.dtype)


def _norm_matmul(x, g, sc, sh, w, three_pass=False, out_dtype=F32):
    b, l, d = x.shape
    n = w.shape[1]
    tm = _pick(l, 256 if three_pass else 512)
    tn = _pick(n, 512, LANES)
    x_spec = pl.BlockSpec((None, tm, d), lambda bi, i, j: (bi, i, 0))
    g_spec = pl.BlockSpec((1, d), lambda bi, i, j: (0, 0))
    m_spec = pl.BlockSpec((None, 1, d), lambda bi, i, j: (bi, 0, 0))
    w_spec = pl.BlockSpec((d, tn), lambda bi, i, j: (0, j))
    o_spec = pl.BlockSpec((None, tm, tn), lambda bi, i, j: (bi, i, j))
    if three_pass:
        w_hi, w_lo = _split2(w)
        kern, w_args, w_specs = _nm3_kernel, (w_hi, w_lo), [w_spec, w_spec]
        scratch = [pltpu.VMEM((tm, d), BF16), pltpu.VMEM((tm, d), BF16)]
    else:
        kern, w_args, w_specs = _nm_kernel, (w.astype(BF16),), [w_spec]
        scratch = [pltpu.VMEM((tm, d), BF16)]
    return pl.pallas_call(
        kern,
        grid=(b, l // tm, n // tn),
        in_specs=[x_spec, g_spec, m_spec, m_spec] + w_specs,
        out_specs=o_spec,
        out_shape=jax.ShapeDtypeStruct((b, l, n), out_dtype),
        scratch_shapes=scratch,
        compiler_params=_params("parallel", "parallel", "arbitrary"),
        name="norm_matmul3" if three_pass else "norm_matmul",
    )(x, g.reshape(1, d), sc, sh, *w_args)


def _resid_mm_kernel(z_ref, w_ref, x_ref, gt_ref, o_ref):
    o_ref[...] = x_ref[...] + gt_ref[...] * _dot(z_ref[...].astype(BF16), w_ref[...])


def _resid_matmul(z, w, x, gate):
    b, l, k = z.shape
    n = w.shape[1]
    tm = _pick(l, 512)
    tn = _pick(n, 512, LANES)
    return pl.pallas_call(
        _resid_mm_kernel,
        grid=(b, l // tm, n // tn),
        in_specs=[pl.BlockSpec((None, tm, k), lambda bi, i, j: (bi, i, 0)),
                  pl.BlockSpec((k, tn), lambda bi, i, j: (0, j)),
                  pl.BlockSpec((None, tm, tn), lambda bi, i, j: (bi, i, j)),
                  pl.BlockSpec((None, 1, tn), lambda bi, i, j: (bi, 0, j))],
        out_specs=pl.BlockSpec((None, tm, tn), lambda bi, i, j: (bi, i, j)),
        out_shape=jax.ShapeDtypeStruct((b, l, n), F32),
        compiler_params=_params("parallel", "parallel", "parallel"),
        name="resid_matmul",
    )(z, w.astype(BF16), x, gate)


def _conv3_kernel(x_ref, p_ref, n_ref, w_ref, o_ref, *, tm, gdn_d, head_scale):
    i = pl.program_id(1)
    last = pl.num_programs(1) - 1
    x = x_ref[...]
    rows = lax.broadcasted_iota(I32, (tm, 1), 0)
    prev_row = jnp.where(i > 0, p_ref[SUBLANES - 1:SUBLANES, :], 0.0)
    next_row = jnp.where(i < last, n_ref[0:1, :], 0.0)
    x_m = jnp.where(rows == 0, prev_row, pltpu.roll(x, 1, axis=0))
    x_p = jnp.where(rows == tm - 1, next_row, pltpu.roll(x, tm - 1, axis=0))
    w = w_ref[...]
    y = x_m * w[0:1, :] + x * w[1:2, :] + x_p * w[2:3, :]
    if not gdn_d:
        o_ref[...] = y
        return
    y = _silu(y)
    sec = (pl.program_id(2) * y.shape[1]) // gdn_d
    for hd in range(y.shape[1] // LANES):
        y_h = y[:, hd * LANES:(hd + 1) * LANES]
        nrm = y_h * lax.rsqrt(jnp.sum(y_h * y_h, axis=-1, keepdims=True) + NORM_EPS)
        nrm = nrm * jnp.where(sec == 0, head_scale, 1.0)
        o_ref[:, hd * LANES:(hd + 1) * LANES] = jnp.where(sec < 2, nrm, y_h)


def _conv3(x, w, n_cols, gdn_d=0):
    b, l, _ = x.shape
    tm = _pick(l, 512)
    tc = _pick(gdn_d if gdn_d else n_cols, 1024, LANES)
    r8 = tm // SUBLANES
    kern = functools.partial(_conv3_kernel, tm=tm, gdn_d=gdn_d, head_scale=float(LANES) ** -0.5)
    return pl.pallas_call(
        kern,
        grid=(b, l // tm, n_cols // tc),
        in_specs=[pl.BlockSpec((None, tm, tc), lambda bi, i, c: (bi, i, c)),
                  pl.BlockSpec((None, SUBLANES, tc), lambda bi, i, c: (bi, jnp.maximum(i * r8 - 1, 0), c)),
                  pl.BlockSpec((None, SUBLANES, tc),
                               lambda bi, i, c: (bi, jnp.minimum((i + 1) * r8, l // SUBLANES - 1), c)),
                  pl.BlockSpec((3, tc), lambda bi, i, c: (0, c))],
        out_specs=pl.BlockSpec((None, tm, tc), lambda bi, i, c: (bi, i, c)),
        out_shape=jax.ShapeDtypeStruct((b, l, n_cols), F32),
        compiler_params=_params("parallel", "parallel", "parallel"),
        name="conv3_gdn" if gdn_d else "conv3",
    )(x, x, x, w)


def _hyfilt_kernel(band_ref, w1_ref, b1_ref, w2_ref, b2_ref, sf_ref, w3_ref, b3_ref, ld_ref,
                   o_ref, ss_ref, hid_ref, *, tl, seq, n_band):
    i = pl.program_id(1)
    r = i * tl + lax.broadcasted_iota(I32, (tl, 1), 0)
    pos = jnp.where(r < seq, r, 2 * seq - r).astype(F32)
    t = pos / float(seq - 1)

    @pl.when(pl.program_id(0) == 0)
    def _():
        omega = (2.0 * math.pi) * pos / float(seq)
        ang = omega * band_ref[...]
        lane = lax.broadcasted_iota(I32, (tl, LANES), 1)
        feats = jnp.where(lane == 0, t,
                          jnp.where(lane <= n_band, jnp.cos(ang),
                                    jnp.where(lane <= 2 * n_band, -jnp.sin(ang), 0.0)))
        sf = sf_ref[...]
        h1 = jnp.sin(sf[0:1, :] * (jnp.dot(feats, w1_ref[...], precision=HIGHEST) + b1_ref[...]))
        hid_ref[i] = jnp.sin(sf[1:2, :] * (jnp.dot(h1, w2_ref[...], precision=HIGHEST) + b2_ref[...]))

    h = jnp.dot(hid_ref[i], w3_ref[...], precision=HIGHEST) + b3_ref[...]
    out = jnp.where(r == seq, 0.0, h * jnp.exp(-t * jnp.exp(ld_ref[...])))
    o_ref[...] = out

    @pl.when(i == 0)
    def _():
        ss_ref[...] = jnp.zeros_like(ss_ref)

    ss_ref[...] += jnp.sum(out * out, axis=0, keepdims=True)


def _hyena_filters(seq, w1, b1, w2, b2, w3, b3, sin_freq, log_decay):
    emb, hid = w1.shape
    n_band = (emb - 1) // 2
    n_order, n_dir, d_model = log_decay.shape
    assert n_dir == 2
    n_out = n_order * d_model
    bands = jnp.linspace(1e-4, n_band - 1, n_band, dtype=F32)
    band_row = jnp.zeros((1, LANES), F32).at[0, 1:1 + n_band].set(bands).at[0, 1 + n_band:1 + 2 * n_band].set(bands)
    w1p = jnp.zeros((LANES, hid), F32).at[:emb].set(w1)
    tl = _pick(seq, 512)
    tn = _pick(d_model, 1024, LANES)
    per_d = d_model // tn
    kern = functools.partial(_hyfilt_kernel, tl=tl, seq=seq, n_band=n_band)
    full = lambda shape: pl.BlockSpec(shape, lambda j, i: (0,) * len(shape))
    src_col = lambda j, i: (0, ((j // per_d) * 2 + (i * tl) // seq) * per_d + j % per_d)
    return pl.pallas_call(
        kern,
        grid=(n_out // tn, 2 * seq // tl),
        in_specs=[full((1, LANES)), full((LANES, hid)), full((1, hid)), full((hid, hid)), full((1, hid)),
                  full((2, hid)),
                  pl.BlockSpec((hid, tn), src_col),
                  pl.BlockSpec((1, tn), src_col),
                  pl.BlockSpec((1, tn), src_col)],
        out_specs=[pl.BlockSpec((tl, tn), lambda j, i: (i, j)),
                   pl.BlockSpec((1, tn), lambda j, i: (0, j))],
        out_shape=[jax.ShapeDtypeStruct((2 * seq, n_out), F32), jax.ShapeDtypeStruct((1, n_out), F32)],
        scratch_shapes=[pltpu.VMEM((2 * seq // tl, tl, hid), F32)],
        compiler_params=_params("arbitrary", "arbitrary"),
        name="hyena_filters",
    )(band_row, w1p, b1.reshape(1, hid), w2, b2.reshape(1, hid), sin_freq, w3, b3.reshape(1, -1),
      log_decay.reshape(1, -1))


def _lmm_kernel(m_ref, x_ref, o_ref, *, gb):
    for g in range(gb):
        o_ref[g] = _dot(m_ref[g], x_ref[g].astype(BF16)).astype(o_ref.dtype)


def _lmm_epi_kernel(m_ref, x_ref, z_ref, gt_ref, bias_ref, o_ref, *, gb):
    for g in range(gb):
        conv = _dot(m_ref[g], x_ref[g].astype(BF16))
        o_ref[g] = gt_ref[g] * (conv + z_ref[g] * bias_ref[...])


def _fft_stage1(mats, x, epilogue=None, out_dtype=F32):
    g_n, r_out, r_in = mats.shape
    d = x.shape[-1]
    gb = _pick(g_n, 4, 1)
    td = _pick(d, 1024, LANES)
    m_spec = pl.BlockSpec((gb, r_out, r_in), lambda g, j: (g, 0, 0))
    x_spec = pl.BlockSpec((gb, r_in, td), lambda g, j: (g, 0, j))
    o_spec = pl.BlockSpec((gb, r_out, td), lambda g, j: (g, 0, j))
    if epilogue is None:
        kern, extra, extra_specs = functools.partial(_lmm_kernel, gb=gb), (), []
    else:
        kern = functools.partial(_lmm_epi_kernel, gb=gb)
        extra = epilogue
        extra_specs = [o_spec, o_spec, pl.BlockSpec((1, td), lambda g, j: (0, j))]
    return pl.pallas_call(
        kern,
        grid=(g_n // gb, d // td),
        in_specs=[m_spec, x_spec] + extra_specs,
        out_specs=o_spec,
        out_shape=jax.ShapeDtypeStruct((g_n, r_out, d), out_dtype),
        compiler_params=_params("parallel", "parallel"),
        name="fft_stage1" if epilogue is None else "fft_stage1_inv",
    )(mats, x, *extra)


def _s2f_kernel(m_ref, x_ref, o_ref, *, gb):
    for g in range(gb):
        o_ref[g] = _dot(m_ref[...], x_ref[g].astype(BF16))


def _s2c_kernel(m_ref, i_ref, x_ref, f_ref, ss_ref, o_ref, *, gb, half):
    scale = lax.rsqrt(ss_ref[...] + NORM_EPS)
    for g in range(gb):
        spec = _dot(m_ref[...], x_ref[g].astype(BF16))
        f = f_ref[g]
        s_re, s_im = spec[:half], spec[half:]
        f_re, f_im = f[:half], f[half:]
        prod = jnp.concatenate([s_re * f_re - s_im * f_im, s_re * f_im + s_im * f_re], axis=0) * scale
        o_ref[g] = _dot(i_ref[...], prod.astype(BF16)).astype(o_ref.dtype)


def _fft_stage2_fwd(mat, x):
    r = mat.shape[0]
    g_n, _, d = x.shape
    gb = _pick(g_n, 4, 1)
    td = _pick(d, 1024, LANES)
    m_spec = pl.BlockSpec((r, r), lambda g, j: (0, 0))
    x_spec = pl.BlockSpec((gb, r, td), lambda g, j: (g, 0, j))
    return pl.pallas_call(
        functools.partial(_s2f_kernel, gb=gb),
        grid=(g_n // gb, d // td),
        in_specs=[m_spec, x_spec],
        out_specs=x_spec,
        out_shape=jax.ShapeDtypeStruct(x.shape, F32),
        compiler_params=_params("parallel", "parallel"),
        name="fft_stage2_fwd",
    )(mat, x)


def _fft_stage2_conv(mat, imat, x, filt_spec, sumsq):
    r = mat.shape[0]
    g_n, _, d = x.shape
    gb = _pick(g_n, 4, 1)
    td = _pick(d, 1024, LANES)
    m_spec = pl.BlockSpec((r, r), lambda g, j: (0, 0))
    x_spec = pl.BlockSpec((gb, r, td), lambda g, j: (g, 0, j))
    return pl.pallas_call(
        functools.partial(_s2c_kernel, gb=gb, half=r // 2),
        grid=(g_n // gb, d // td),
        in_specs=[m_spec, m_spec, x_spec, x_spec, pl.BlockSpec((1, td), lambda g, j: (0, j))],
        out_specs=x_spec,
        out_shape=jax.ShapeDtypeStruct(x.shape, BF16),
        compiler_params=_params("parallel", "parallel"),
        name="fft_stage2_conv",
    )(mat, imat, x, filt_spec, sumsq)


def _dft_tables(seq):
    n = 2 * seq
    n2 = FFT_N2
    n1 = n // n2
    n1h = n1 // 2
    k1 = jnp.arange(n1, dtype=I32)
    n2i = jnp.arange(n2, dtype=I32)
    n1i = jnp.arange(n1, dtype=I32)
    pos = n1i[None, :] * n2 + n2i[:, None]
    prod = (k1[None, :, None] * pos[:, None, :]) % n
    ang = prod.astype(F32) * (2.0 * math.pi / n)
    c, s = jnp.cos(ang), jnp.sin(ang)
    ch, sh = c[:, :, :n1h], s[:, :, :n1h]
    fwd_data = jnp.concatenate([jnp.concatenate([ch, sh], axis=2),
                                jnp.concatenate([-sh, ch], axis=2)], axis=1)
    fwd_real = jnp.concatenate([c, -s], axis=1)
    ct, st = jnp.swapaxes(ch, 1, 2), jnp.swapaxes(sh, 1, 2)
    inv_data = jnp.concatenate([jnp.concatenate([ct, -st], axis=2),
                                jnp.concatenate([st, ct], axis=2)], axis=1) * (1.0 / n)
    a2 = ((n2i[:, None] * n2i[None, :]) % n2).astype(F32) * (2.0 * math.pi / n2)
    c2, s2 = jnp.cos(a2), jnp.sin(a2)
    m2 = jnp.concatenate([jnp.concatenate([c2, s2], axis=1), jnp.concatenate([-s2, c2], axis=1)], axis=0)
    m2i = jnp.concatenate([jnp.concatenate([c2, -s2], axis=1), jnp.concatenate([s2, c2], axis=1)], axis=0)
    bf = lambda a: a.astype(BF16)
    return dict(n1=n1, n1h=n1h, fwd_data=bf(fwd_data), fwd_real=bf(fwd_real), inv_data=bf(inv_data),
                m2=bf(m2), m2i=bf(m2i))


def _to_stage2_layout(a, n1):
    n2, r, d = a.shape
    c = r // n1
    return a.reshape(n2, c, n1, d).transpose(2, 1, 0, 3).reshape(n1, c * n2, d)


def _to_stage1_layout(a, n2):
    n1, r, d = a.shape
    c = r // n2
    return a.reshape(n1, c, n2, d).transpose(2, 1, 0, 3).reshape(n2, c * n1, d)


def _hyena_core(u, two_sided, sumsq, bias):
    b, seq, d3 = u.shape
    d = d3 // 3
    n_order = bias.shape[0]
    assert b == 2
    tab = _dft_tables(seq)
    n1, n1h, n2 = tab["n1"], tab["n1h"], FFT_N2
    ut = u.reshape(b, n1h, n2, 3, d).transpose(3, 2, 0, 1, 4).reshape(3, n2, b * n1h, d)
    f_t = two_sided.reshape(n1, n2, n_order, d).transpose(2, 1, 0, 3)
    z_t = ut[0]
    for o in range(n_order):
        f_half = _fft_stage1(tab["fwd_real"], f_t[o], out_dtype=BF16)
        f_spec = _fft_stage2_fwd(tab["m2"], _to_stage2_layout(f_half, n1))
        ss = sumsq.reshape(n_order, 1, d)[o]
        a = _to_stage2_layout(_fft_stage1(tab["fwd_data"], z_t, out_dtype=BF16), n1)
        c = _to_stage1_layout(_fft_stage2_conv(tab["m2"], tab["m2i"], a, f_spec, ss), n2)
        z_t = _fft_stage1(tab["inv_data"], c, epilogue=(z_t, ut[1 + o], bias[o].reshape(1, d)))
    return z_t.reshape(n2, b, n1h, d).transpose(1, 2, 0, 3).reshape(b, seq, d)


def _hyena_mixer(x, g, sc, sh, gate, params):
    w_in, conv, w1, b1, w2, b2, w3, b3, sin_freq, log_decay, bias, w_out = params
    seq = x.shape[1]
    y = _norm_matmul(x, g, sc, sh, w_in)
    u = _conv3(y, conv, y.shape[-1])
    filt, sumsq = _hyena_filters(seq, w1, b1, w2, b2, w3, b3, sin_freq, log_decay)
    z = _hyena_core(u, filt, sumsq, bias)
    return _resid_matmul(z, w_out, x, gate)


def _pool_kernel(x_ref, p_ref, n_ref, o_ref, *, tm, seq, group):
    i = pl.program_id(1)
    last = pl.num_programs(1) - 1
    x = x_ref[...]
    prev = jnp.where(i > 0, p_ref[...], 0.0)
    nxt = jnp.where(i < last, n_ref[...], 0.0)
    ext = jnp.concatenate([prev, x, nxt], axis=0)
    gi = (pl.program_id(2) * x.shape[1]) // group
    half = jnp.left_shift(1, gi)
    r = lax.broadcasted_iota(I32, (tm, tm + 2 * POOL_HALO), 0)
    c = lax.broadcasted_iota(I32, (tm, tm + 2 * POOL_HALO), 1)
    band = jnp.logical_and(c >= r + POOL_HALO - half, c < r + POOL_HALO + half).astype(F32)
    win = _dot_exact_lhs(band, ext)
    t = i * tm + lax.broadcasted_iota(I32, (tm, 1), 0)
    cnt = (jnp.minimum(t + half, seq) - jnp.maximum(t - half, 0)).astype(F32)
    o_ref[...] = win / cnt - x


def _pool_windows(uf):
    b, l, d = uf.shape
    group = d // len(POOL_WINDOWS)
    tm = _pick(l, 256)
    tc = _pick(group, 512, LANES)
    r8 = tm // SUBLANES
    return pl.pallas_call(
        functools.partial(_pool_kernel, tm=tm, seq=l, group=group),
        grid=(b, l // tm, d // tc),
        in_specs=[pl.BlockSpec((None, tm, tc), lambda bi, i, c: (bi, i, c)),
                  pl.BlockSpec((None, SUBLANES, tc), lambda bi, i, c: (bi, jnp.maximum(i * r8 - 1, 0), c)),
                  pl.BlockSpec((None, SUBLANES, tc),
                               lambda bi, i, c: (bi, jnp.minimum((i + 1) * r8, l // SUBLANES - 1), c))],
        out_specs=pl.BlockSpec((None, tm, tc), lambda bi, i, c: (bi, i, c)),
        out_shape=jax.ShapeDtypeStruct((b, l, d), F32),
        compiler_params=_params("parallel", "parallel", "parallel"),
        name="pool_windows",
    )(uf, uf, uf)


def _group_mm_kernel(p_ref, w_ref, s_ref, o_ref):
    o_ref[...] = (_dot(p_ref[...].astype(BF16), w_ref[...]) * s_ref[...]).astype(o_ref.dtype)


def _group_matmul(p, w_grp, scale):
    b, l, d = p.shape
    n_g, gd, _ = w_grp.shape
    tm = _pick(l, 512)
    tn = _pick(gd, 512, LANES)
    per = gd // tn
    return pl.pallas_call(
        _group_mm_kernel,
        grid=(b, l // tm, n_g, per),
        in_specs=[pl.BlockSpec((None, tm, gd), lambda bi, i, g, j: (bi, i, g)),
                  pl.BlockSpec((None, gd, tn), lambda bi, i, g, j: (g, 0, j)),
                  pl.BlockSpec((1, tn), lambda bi, i, g, j: (0, g * per + j))],
        out_specs=pl.BlockSpec((None, tm, tn), lambda bi, i, g, j: (bi, i, g * per + j)),
        out_shape=jax.ShapeDtypeStruct((b, l, d), BF16),
        compiler_params=_params("parallel", "parallel", "parallel", "parallel"),
        name="pool_group_matmul",
    )(p, w_grp.astype(BF16), scale.reshape(1, d))


def _pool_mixer(x, g, sc, sh, gate, params):
    w_in, w_grp, scale, w_out = params
    uf = _norm_matmul(x, g, sc, sh, w_in)
    y = _group_matmul(_pool_windows(uf), w_grp, scale)
    return _resid_matmul(y, w_out, x, gate)


def _gdn_gates_kernel(x_ref, alog_ref, dtb_ref, o_ref, *, n_heads):
    x = x_ref[...]
    lane = lax.broadcasted_iota(I32, x.shape, 1)
    decay = -jnp.exp(alog_ref[...]) * jax.nn.softplus(x + dtb_ref[...])
    o_ref[...] = jnp.where(lane < 2 * n_heads, jax.nn.sigmoid(x), decay)


def _gdn_gates(proj, col_block, a_log, dt_bias):
    b, l, _ = proj.shape
    n_heads = a_log.shape[1]
    pad = jnp.zeros((1, 2 * n_heads), F32)
    alog_row = jnp.concatenate([pad, a_log.reshape(1, 2 * n_heads)], axis=1)
    dtb_row = jnp.concatenate([pad, dt_bias.reshape(1, 2 * n_heads)], axis=1)
    tm = _pick(l, 1024)
    w = 4 * n_heads
    return pl.pallas_call(
        functools.partial(_gdn_gates_kernel, n_heads=n_heads),
        grid=(b, l // tm),
        in_specs=[pl.BlockSpec((None, tm, w), lambda bi, i: (bi, i, col_block)),
                  pl.BlockSpec((1, w), lambda bi, i: (0, 0)),
                  pl.BlockSpec((1, w), lambda bi, i: (0, 0))],
        out_specs=pl.BlockSpec((None, tm, w), lambda bi, i: (bi, i, 0)),
        out_shape=jax.ShapeDtypeStruct((b, l, w), F32),
        compiler_params=_params("parallel", "parallel"),
        name="gdn_gates",
    )(proj, alog_row, dtb_row)


def _unit_triangular_inverse(a, row, col):
    n = a[0].shape[0]
    eye = (row == col).astype(F32)
    blk = lambda s: (row // s) == (col // s)
    a8 = [jnp.where(blk(8), x, 0.0) for x in a]
    a8_2 = _each(_dot_x3, a8, a8)
    a8_4 = _each(_dot_x3, a8_2, a8_2)
    t = _each(_dot_x3, [eye - x for x in a8], [eye + x for x in a8_2])
    t = _each(_dot_x3, t, [eye + x for x in a8_4])
    s = 8
    while s < n:
        mask = jnp.logical_and(blk(2 * s), jnp.logical_not(blk(s)))
        off = [jnp.where(mask, x, 0.0) for x in a]
        corr = _each(_dot_x3, _each(_dot_x3, t, off), t)
        t = [x - y for x, y in zip(t, corr)]
        s *= 2
    return t


def _each(fn, *lists):
    return [fn(*xs) for xs in zip(*lists)]


def _gdn_chunks(chains, head, row, col, lane):
    c_n = GDN_CHUNK
    q, k, v, gb, s_mat, rev, beta0, g0 = (list(z) for z in zip(*chains))
    incl = [(row <= col) if r else (row >= col) for r in rev]
    strict = [(row < col) if r else (row > col) for r in rev]
    edge = [0 if r else c_n - 1 for r in rev]
    eye = row == col
    ones = jnp.ones((c_n, c_n), F32)
    beta = [jnp.sum(jnp.where(lane == b0 + head, x, 0.0), axis=1, keepdims=True) for x, b0 in zip(gb, beta0)]
    g = [jnp.sum(jnp.where(lane == b0 + head, x, 0.0), axis=1, keepdims=True) for x, b0 in zip(gb, g0)]
    gc = [_dot_exact_lhs(m.astype(F32), jnp.broadcast_to(x, (c_n, LANES))) for m, x in zip(incl, g)]
    gc_col = [x[:, :c_n] for x in gc]
    gc_row = [_dot_exact_lhs(ones, jnp.where(eye, x, 0.0)) for x in gc_col]
    decay = [jnp.where(m, jnp.exp(jnp.where(m, c - r, 0.0)), 0.0) for m, c, r in zip(incl, gc_col, gc_row)]
    kb = [x * b for x, b in zip(k, beta)]
    vb = [x * b for x, b in zip(v, beta)]
    k16 = [x.astype(BF16) for x in k]
    kk = _each(_dot_nt, [x.astype(BF16) for x in kb], k16)
    a_kk = [jnp.where(m, x * d, 0.0) for m, x, d in zip(strict, kk, decay)]
    t_inv = _unit_triangular_inverse(a_kk, row, col)
    e_gc = [jnp.exp(x) for x in gc]
    rhs = [jnp.concatenate([x, y * e], axis=1) for x, y, e in zip(vb, kb, e_gc)]
    sol = _each(_dot_x3, t_inv, rhs)
    qk = _each(_dot_nt, [x.astype(BF16) for x in q], k16)
    a_qk = [(x * d).astype(BF16) for x, d in zip(qk, decay)]
    gc_edge = [x[e:e + 1, :] for x, e in zip(gc, edge)]
    q_s = [(x * e).astype(BF16) for x, e in zip(q, e_gc)]
    k_tail = [(x * jnp.exp(ge - c)).astype(BF16) for x, ge, c in zip(k, gc_edge, gc)]
    s16 = [x.astype(BF16) for x in s_mat]
    ws_s = _each(_dot, [x[:, LANES:].astype(BF16) for x in sol], s16)
    u16 = [(x[:, :LANES] - y).astype(BF16) for x, y in zip(sol, ws_s)]
    o_state = _each(_dot, q_s, s16)
    o_local = _each(_dot, a_qk, u16)
    s_add = _each(_dot_tn, k_tail, u16)
    o = [x + y for x, y in zip(o_state, o_local)]
    s_new = [x * jnp.exp(ge) + y for x, ge, y in zip(s_mat, gc_edge, s_add)]
    return o, s_new


def _gdn_scan_kernel(qf_ref, kf_ref, vf_ref, gf_ref, qb_ref, kb_ref, vb_ref, gb_ref, of_ref, ob_ref, s_ref,
                     *, n_chunks, n_batch, n_heads):
    head = pl.program_id(0) * GDN_HEADS_PER_STEP

    @pl.when(pl.program_id(1) == 0)
    def _():
        s_ref[...] = jnp.zeros_like(s_ref)

    c_n = GDN_CHUNK
    row = lax.broadcasted_iota(I32, (c_n, c_n), 0)
    col = lax.broadcasted_iota(I32, (c_n, c_n), 1)
    lane = lax.broadcasted_iota(I32, (c_n, LANES), 1)
    dirs = ((qf_ref, kf_ref, vf_ref, gf_ref, of_ref, False, 0, 2 * n_heads),
            (qb_ref, kb_ref, vb_ref, gb_ref, ob_ref, True, n_heads, 3 * n_heads))

    def chunk(ci, carry):
        chains, dests = [], []
        for di, (q_ref, k_ref, v_ref, g_ref, o_ref, rev, beta0, g0) in enumerate(dirs):
            cj = (n_chunks - 1 - ci) if rev else ci
            r0 = pl.multiple_of(cj * c_n, c_n)
            for bi in range(n_batch):
                gates = g_ref[bi, pl.ds(r0, c_n), :]
                for hh in range(GDN_HEADS_PER_STEP):
                    si = (di * n_batch + bi) * GDN_HEADS_PER_STEP + hh
                    cols = slice(hh * LANES, (hh + 1) * LANES)
                    chains.append((q_ref[bi, pl.ds(r0, c_n), cols], k_ref[bi, pl.ds(r0, c_n), cols],
                                   v_ref[bi, pl.ds(r0, c_n), cols], gates, s_ref[si],
                                   rev, beta0 + hh, g0 + hh))
                    dests.append((o_ref, bi, r0, cols, si))
        o, s_new = _gdn_chunks(chains, head, row, col, lane)
        for o_c, s_c, (o_ref, bi, r0, cols, si) in zip(o, s_new, dests):
            s_ref[si] = s_c
            o_ref[bi, pl.ds(r0, c_n), cols] = o_c
        return carry

    lax.fori_loop(0, n_chunks, chunk, 0)


def _gdn_scan(qkv, gb, d_model):
    b, l, _ = qkv.shape
    n_heads = d_model // LANES
    tl = _pick(l, 512, GDN_CHUNK)
    n_l = l // tl
    kern = functools.partial(_gdn_scan_kernel, n_chunks=tl // GDN_CHUNK, n_batch=b, n_heads=n_heads)
    hps = GDN_HEADS_PER_STEP
    assert n_heads % hps == 0
    n_hb = n_heads // hps
    wide = hps * LANES
    fwd = lambda sec: pl.BlockSpec((b, tl, wide), lambda h, i: (0, i, sec * n_hb + h))
    bwd = lambda sec: pl.BlockSpec((b, tl, wide), lambda h, i: (0, n_l - 1 - i, sec * n_hb + h))
    gate_f = pl.BlockSpec((b, tl, gb.shape[-1]), lambda h, i: (0, i, 0))
    gate_b = pl.BlockSpec((b, tl, gb.shape[-1]), lambda h, i: (0, n_l - 1 - i, 0))
    out_sd = jax.ShapeDtypeStruct((b, l, d_model), F32)
    return pl.pallas_call(
        kern,
        grid=(n_hb, n_l),
        in_specs=[fwd(0), fwd(1), fwd(2), gate_f, bwd(0), bwd(1), bwd(2), gate_b],
        out_specs=[pl.BlockSpec((b, tl, wide), lambda h, i: (0, i, h)),
                   pl.BlockSpec((b, tl, wide), lambda h, i: (0, n_l - 1 - i, h))],
        out_shape=[out_sd, out_sd],
        scratch_shapes=[pltpu.VMEM((2 * b * hps, LANES, LANES), F32)],
        compiler_params=_params("parallel", "arbitrary"),
        name="gdn_scan",
    )(qkv, qkv, qkv, gb, qkv, qkv, qkv, gb)


def _gdn_out_kernel(of_ref, ob_ref, gt_ref, w_ref, o_ref):
    o = of_ref[...] + ob_ref[...]
    ms = jnp.mean(o * o, axis=-1, keepdims=True)
    y = o * lax.rsqrt(ms + NORM_EPS) * w_ref[...]
    o_ref[...] = (y * _silu(gt_ref[...])).astype(o_ref.dtype)


def _gdn_out(o_f, o_b, proj, gate_block0, o_norm):
    b, l, d = o_f.shape
    tm = _pick(l, 1024)
    spec = pl.BlockSpec((None, tm, LANES), lambda bi, i, h: (bi, i, h))
    return pl.pallas_call(
        _gdn_out_kernel,
        grid=(b, l // tm, d // LANES),
        in_specs=[spec, spec,
                  pl.BlockSpec((None, tm, LANES), lambda bi, i, h: (bi, i, gate_block0 + h)),
                  pl.BlockSpec((1, LANES), lambda bi, i, h: (0, 0))],
        out_specs=spec,
        out_shape=jax.ShapeDtypeStruct((b, l, d), BF16),
        compiler_params=_params("parallel", "parallel", "parallel"),
        name="gdn_out",
    )(o_f, o_b, proj, o_norm.reshape(1, LANES))


def _gdn_mixer(x, g, sc, sh, gate, params):
    w_in, conv, a_log, dt_bias, o_norm, w_out = params
    d = x.shape[-1]
    n_heads = a_log.shape[1]
    assert d // n_heads == LANES and 4 * n_heads == LANES
    proj = _norm_matmul(x, g, sc, sh, w_in)
    qkv = _conv3(proj, conv, 3 * d, gdn_d=d)
    gb = _gdn_gates(proj, (4 * d) // LANES, a_log, dt_bias)
    o_f, o_b = _gdn_scan(qkv, gb, d)
    y = _gdn_out(o_f, o_b, proj, (3 * d) // LANES, o_norm)
    return _resid_matmul(y, w_out, x, gate)


def _top_rows(s, k, payload=None):
    r_n = s.shape[0]
    row = lax.broadcasted_iota(I32, s.shape, 0)
    vals, idxs, pays = [], [], []
    for _ in range(k):
        m = jnp.max(s, axis=0, keepdims=True)
        idx = jnp.min(jnp.where(s == m, row, r_n), axis=0, keepdims=True)
        hit = row == idx
        vals.append(m)
        idxs.append(idx)
        if payload is not None:
            pays.append(jnp.max(jnp.where(hit, payload, -1), axis=0, keepdims=True))
        s = jnp.where(hit, -jnp.inf, s)
    out = (jnp.concatenate(vals, axis=0), jnp.concatenate(idxs, axis=0))
    if payload is not None:
        out += (jnp.concatenate(pays, axis=0),)
    return out


def _peer_route_kernel(q_ref, khi_ref, klo_ref, ids_ref, gates_ref, *, n_heads, n_keys, key_dim):
    k_top = PEER_TOPK
    for hd in range(n_heads):
        tops = []
        for p in range(2):
            c0 = (hd * 2 + p) * key_dim
            q_hi, q_lo = _split2(q_ref[:, c0:c0 + key_dim])
            s = _dot3(khi_ref[p], klo_ref[p], q_hi, q_lo, _dot_nt)
            tops.append(_top_rows(s, k_top))
        (s0, i0), (s1, i1) = tops
        keep = [min(k_top, -(-(k_top // (a + 1)) // SUBLANES) * SUBLANES) for a in range(k_top)]
        cand = jnp.concatenate([s0[a:a + 1, :] + s1[:keep[a]] for a in range(k_top)], axis=0)
        cid = jnp.concatenate([i0[a:a + 1, :] * n_keys + i1[:keep[a]] for a in range(k_top)], axis=0)
        best, _, ids = _top_rows(cand, k_top, payload=cid)
        e = jnp.exp(best - best[0:1, :])
        gates = e / jnp.sum(e, axis=0, keepdims=True)
        ids_ref[hd * k_top:(hd + 1) * k_top, :] = ids
        gates_ref[hd * k_top:(hd + 1) * k_top, :] = gates


def _peer_route(q, keys):
    t_n, qd = q.shape
    _, n_keys, key_dim = keys.shape
    n_heads = qd // (2 * key_dim)
    tm = _pick(t_n, 256, LANES)
    k_hi, k_lo = _split2(keys)
    kern = functools.partial(_peer_route_kernel, n_heads=n_heads, n_keys=n_keys, key_dim=key_dim)
    rows = n_heads * PEER_TOPK
    k_spec = pl.BlockSpec((2, n_keys, key_dim), lambda i: (0, 0, 0))
    o_spec = pl.BlockSpec((rows, tm), lambda i: (0, i))
    return pl.pallas_call(
        kern,
        grid=(t_n // tm,),
        in_specs=[pl.BlockSpec((tm, qd), lambda i: (i, 0)), k_spec, k_spec],
        out_specs=[o_spec, o_spec],
        out_shape=[jax.ShapeDtypeStruct((rows, t_n), I32), jax.ShapeDtypeStruct((rows, t_n), F32)],
        compiler_params=_params("parallel"),
        name="peer_route",
    )(q, k_hi, k_lo)


def _peer_gather_kernel(ids_ref, gates_ref, x_ref, g_ref, sc_ref, sh_ref, og_ref, uv_hbm,
                        o_ref, h_ref, buf, sem, *, tb, n_sel, n_slot, rows, grp):
    x = x_ref[...]
    ms = jnp.sum(jnp.sum(x * x, axis=2, keepdims=True), axis=1, keepdims=True) * (1.0 / (rows * LANES))
    h_ref[...] = (x * lax.rsqrt(ms + NORM_EPS) * g_ref[...]) * (1.0 + sc_ref[...]) + sh_ref[...]

    n_q = rows // grp
    jc_n = SUBLANES
    sel = (lax.broadcasted_iota(I32, (n_sel, n_sel * grp), 1) // grp
           == lax.broadcasted_iota(I32, (n_sel, n_sel * grp), 0)).astype(BF16)
    sel_t = (lax.broadcasted_iota(I32, (n_sel * grp, n_sel), 0) // grp
             == lax.broadcasted_iota(I32, (n_sel * grp, n_sel), 1)).astype(BF16)
    lane_t = lax.broadcasted_iota(I32, (n_sel, tb), 1)

    n_chunk = n_sel // jc_n

    def copies(t, slot):
        def start(j):
            pltpu.make_async_copy(uv_hbm.at[ids_ref[t, j]], buf.at[slot, j], sem.at[slot]).start()
        return [functools.partial(start, j) for j in range(n_sel)]

    def wait(slot):
        pltpu.make_async_copy(uv_hbm.at[pl.ds(0, n_sel)], buf.at[slot], sem.at[slot]).wait()

    def scores(t, slot, tick):
        h = h_ref[t]
        h_q = [h[qi * grp:(qi + 1) * grp] for qi in range(n_q)]
        parts = []
        for jc in range(n_chunk):
            ub = buf[slot, jc * jc_n:(jc + 1) * jc_n, 0:rows, :].astype(F32)
            q = ub[:, 0:grp] * h_q[0]
            for qi in range(1, n_q):
                q = q + ub[:, qi * grp:(qi + 1) * grp] * h_q[qi]
            parts.append(q)
            tick()
        q2 = jnp.concatenate(parts, axis=0).reshape(n_sel * grp, LANES)
        part = _dot(sel, q2.astype(BF16))
        act = jnp.sum(part, axis=1, keepdims=True)
        gate_col = jnp.sum(jnp.where(lane_t == t, gates_ref[...], 0.0), axis=1, keepdims=True)
        wgt = gate_col * (0.5 * act * (1.0 + lax.erf(act * (2.0 ** -0.5))))
        w_b = jnp.broadcast_to(wgt, (n_sel, LANES)).astype(BF16)
        return _dot(sel_t, w_b).reshape(n_sel, grp, LANES)

    def combine(t, slot, w_rep, tick):
        acc = [None] * n_q
        for jc in range(n_chunk):
            vb = buf[slot, jc * jc_n:(jc + 1) * jc_n, rows:2 * rows, :].astype(F32)
            ws = w_rep[jc * jc_n:(jc + 1) * jc_n]
            for qi in range(n_q):
                term = jnp.sum(vb[:, qi * grp:(qi + 1) * grp] * ws, axis=0)
                acc[qi] = term if acc[qi] is None else acc[qi] + term
            tick()
        y = jnp.concatenate(acc, axis=0) if n_q > 1 else acc[0]
        return x_ref[t] + og_ref[...] * y

    def pair(p, slot_pair, prefetch):
        ta, tb_ = 2 * p, 2 * p + 1
        sa, sb = 2 * slot_pair, 2 * slot_pair + 1
        pending = []
        if prefetch:
            nxt_pair = (slot_pair + n_pair - 1) % n_pair
            tn = 2 * (p + n_pair - 1)
            pending = copies(tn, 2 * nxt_pair) + copies(tn + 1, 2 * nxt_pair + 1)
        per_tick = -(-len(pending) // (4 * n_chunk))

        def tick():
            for _ in range(min(per_tick, len(pending))):
                pending.pop(0)()

        wait(sa)
        wait(sb)
        w_a = scores(ta, sa, tick)
        w_b = scores(tb_, sb, tick)
        out_a = combine(ta, sa, w_a, tick)
        out_b = combine(tb_, sb, w_b, tick)
        assert not pending
        o_ref[ta] = out_a
        o_ref[tb_] = out_b

    n_pair = n_slot // 2
    n_group = (tb // 2) // n_pair
    for sp in range(n_pair - 1):
        for start in copies(2 * sp, 2 * sp) + copies(2 * sp + 1, 2 * sp + 1):
            start()

    def group(gi, carry):
        for sp in range(n_pair):
            pair(gi * n_pair + sp, sp, True)
        return carry

    lax.fori_loop(0, n_group - 1, group, 0)
    for sp in range(n_pair):
        pair((n_group - 1) * n_pair + sp, sp, sp == 0)


def _peer_gather(ids, gates_t, x, g, sc, sh, gate, u_tab, v_tab):
    b, l, d = x.shape
    t_n, n_sel = ids.shape
    rows = d // LANES
    grp = min(SUBLANES, rows)
    tb = _pick(l, PEER_TOKENS_PER_STEP, LANES)
    assert tb % PEER_SLOTS == 0 and PEER_SLOTS >= 4 and n_sel % SUBLANES == 0
    per_b = l // tb
    tiles = lambda a: a.reshape(a.shape[0], rows, LANES)
    uv = jnp.concatenate([tiles(u_tab.astype(BF16)), tiles(v_tab.astype(BF16))], axis=1)
    mod_spec = pl.BlockSpec((None, rows, LANES), lambda i: (i // per_b, 0, 0))
    kern = functools.partial(_peer_gather_kernel, tb=tb, n_sel=n_sel, n_slot=PEER_SLOTS, rows=rows, grp=grp)
    out = pl.pallas_call(
        kern,
        grid=(t_n // tb,),
        in_specs=[pl.BlockSpec((tb, n_sel), lambda i: (i, 0), memory_space=pltpu.SMEM),
                  pl.BlockSpec((n_sel, tb), lambda i: (0, i)),
                  pl.BlockSpec((tb, rows, LANES), lambda i: (i, 0, 0)),
                  pl.BlockSpec((rows, LANES), lambda i: (0, 0)),
                  mod_spec, mod_spec, mod_spec,
                  pl.BlockSpec(memory_space=pl.ANY)],
        out_specs=pl.BlockSpec((tb, rows, LANES), lambda i: (i, 0, 0)),
        out_shape=jax.ShapeDtypeStruct((t_n, rows, LANES), F32),
        scratch_shapes=[pltpu.VMEM((tb, rows, LANES), F32),
                        pltpu.VMEM((PEER_SLOTS, n_sel, 2 * rows, LANES), BF16),
                        pltpu.SemaphoreType.DMA((PEER_SLOTS,))],
        compiler_params=_params("arbitrary"),
        name="peer_gather",
    )(ids, gates_t, x.reshape(t_n, rows, LANES), g.reshape(rows, LANES), tiles(sc), tiles(sh), tiles(gate), uv)
    return out.reshape(b, l, d)


def _peer(x, g, sc, sh, gate, params):
    w_q, keys, u_tab, v_tab = params
    b, l, d = x.shape
    q = _norm_matmul(x, g, sc, sh, w_q, three_pass=True)
    ids_t, gates_t = _peer_route(q.reshape(b * l, -1), keys)
    return _peer_gather(ids_t.T, gates_t, x, g, sc, sh, gate, u_tab, v_tab)


def _final_norm_kernel(x_ref, g_ref, o_ref):
    x = x_ref[...]
    ms = jnp.mean(x * x, axis=-1, keepdims=True)
    o_ref[...] = x * lax.rsqrt(ms + NORM_EPS) * g_ref[...]


def _final_norm(x, g):
    b, l, d = x.shape
    tm = _pick(l, 256)
    return pl.pallas_call(
        _final_norm_kernel,
        grid=(b, l // tm),
        in_specs=[pl.BlockSpec((None, tm, d), lambda bi, i: (bi, i, 0)),
                  pl.BlockSpec((1, d), lambda bi, i: (0, 0))],
        out_specs=pl.BlockSpec((None, tm, d), lambda bi, i: (bi, i, 0)),
        out_shape=jax.ShapeDtypeStruct((b, l, d), F32),
        compiler_params=_params("parallel", "parallel"),
        name="final_norm",
    )(x, g.reshape(1, d))


_MIXERS = (_hyena_mixer, _pool_mixer, _gdn_mixer)


def _trunk(x, mods, layers, final_norm):
    d = x.shape[-1]
    for li, (norm_tok, norm_ch, mixer_params, peer_params) in enumerate(layers):
        mod = mods[li].reshape(x.shape[0], 1, N_MOD, d)
        sh_t, sc_t, g_t, sh_c, sc_c, g_c = (mod[:, :, m] for m in range(N_MOD))
        x = _MIXERS[li % len(_MIXERS)](x, norm_tok, sc_t, sh_t, g_t, mixer_params)
        x = _peer(x, norm_ch, sc_c, sh_c, g_c, peer_params)
    return _final_norm(x, final_norm)


def kernel(x_prompt, x_sample, c_prompt, c_sample, l0_ada_w, l0_ada_b, l0_norm_tok, l0_norm_ch, l0_hy_w_in, l0_hy_conv, l0_hy_ffn_w1, l0_hy_ffn_b1, l0_hy_ffn_w2, l0_hy_ffn_b2, l0_hy_ffn_w3, l0_hy_ffn_b3, l0_hy_sin_freq, l0_hy_log_decay, l0_hy_bias, l0_hy_w_out, l0_peer_w_q, l0_peer_keys, l0_peer_u, l0_peer_v, l1_ada_w, l1_ada_b, l1_norm_tok, l1_norm_ch, l1_pool_w_in, l1_pool_w_grp, l1_pool_scale, l1_pool_w_out, l1_peer_w_q, l1_peer_keys, l1_peer_u, l1_peer_v, l2_ada_w, l2_ada_b, l2_norm_tok, l2_norm_ch, l2_gdn_w_in, l2_gdn_conv, l2_gdn_A_log, l2_gdn_dt_bias, l2_gdn_o_norm, l2_gdn_w_out, l2_peer_w_q, l2_peer_keys, l2_peer_u, l2_peer_v, l3_ada_w, l3_ada_b, l3_norm_tok, l3_norm_ch, l3_hy_w_in, l3_hy_conv, l3_hy_ffn_w1, l3_hy_ffn_b1, l3_hy_ffn_w2, l3_hy_ffn_b2, l3_hy_ffn_w3, l3_hy_ffn_b3, l3_hy_sin_freq, l3_hy_log_decay, l3_hy_bias, l3_hy_w_out, l3_peer_w_q, l3_peer_keys, l3_peer_u, l3_peer_v, final_norm):
    layers = (
        (l0_norm_tok, l0_norm_ch,
         (l0_hy_w_in, l0_hy_conv, l0_hy_ffn_w1, l0_hy_ffn_b1, l0_hy_ffn_w2, l0_hy_ffn_b2,
          l0_hy_ffn_w3, l0_hy_ffn_b3, l0_hy_sin_freq, l0_hy_log_decay, l0_hy_bias, l0_hy_w_out),
         (l0_peer_w_q, l0_peer_keys, l0_peer_u, l0_peer_v)),
        (l1_norm_tok, l1_norm_ch,
         (l1_pool_w_in, l1_pool_w_grp, l1_pool_scale, l1_pool_w_out),
         (l1_peer_w_q, l1_peer_keys, l1_peer_u, l1_peer_v)),
        (l2_norm_tok, l2_norm_ch,
         (l2_gdn_w_in, l2_gdn_conv, l2_gdn_A_log, l2_gdn_dt_bias, l2_gdn_o_norm, l2_gdn_w_out),
         (l2_peer_w_q, l2_peer_keys, l2_peer_u, l2_peer_v)),
        (l3_norm_tok, l3_norm_ch,
         (l3_hy_w_in, l3_hy_conv, l3_hy_ffn_w1, l3_hy_ffn_b1, l3_hy_ffn_w2, l3_hy_ffn_b2,
          l3_hy_ffn_w3, l3_hy_ffn_b3, l3_hy_sin_freq, l3_hy_log_decay, l3_hy_bias, l3_hy_w_out),
         (l3_peer_w_q, l3_peer_keys, l3_peer_u, l3_peer_v)),
    )
    ada = ((l0_ada_w, l0_ada_b), (l1_ada_w, l1_ada_b), (l2_ada_w, l2_ada_b), (l3_ada_w, l3_ada_b))
    n_p = c_prompt.shape[0]
    n_s = c_sample.shape[0]
    pad = (-(n_p + n_s)) % SUBLANES
    c_rows = jnp.concatenate([c_prompt, c_sample, jnp.zeros((pad, c_prompt.shape[1]), F32)], axis=0)
    mods = [_ada(c_rows, w, b) for w, b in ada]
    y_prompt = _trunk(x_prompt, [m[:n_p] for m in mods], layers, final_norm)
    y_sample = _trunk(x_sample, [m[n_p:n_p + n_s] for m in mods], layers, final_norm)
    return (y_prompt, y_sample)
```

```python
import functools
import math

import jax
import jax.numpy as jnp
from jax import lax
from jax.experimental import pallas as pl
from jax.experimental.pallas import tpu as pltpu

F32 = jnp.float32
BF16 = jnp.bfloat16
I32 = jnp.int32
HIGHEST = lax.Precision.HIGHEST

NORM_EPS = 1e-6
N_MOD = 6
LANES = 128
SUBLANES = 8
VMEM_LIMIT_BYTES = 56 * 1024 * 1024
FFT_N2 = 128
POOL_WINDOWS = (2, 4, 8, 16)
POOL_HALO = 8
GDN_CHUNK = 64
GDN_HEADS_PER_STEP = 4
PEER_TOPK = 16
PEER_TOKENS_PER_STEP = 256
PEER_SLOTS = 8


def _pick(n, pref, mult=SUBLANES):
    t = (min(pref, n) // mult) * mult
    while t >= mult:
        if n % t == 0:
            return t
        t -= mult
    return n


def _params(*sem):
    return pltpu.CompilerParams(dimension_semantics=sem, vmem_limit_bytes=VMEM_LIMIT_BYTES)


def _split2(x):
    hi = x.astype(BF16)
    lo = (x - hi.astype(F32)).astype(BF16)
    return hi, lo


def _split3(x):
    hi = x.astype(BF16)
    r = x - hi.astype(F32)
    mid = r.astype(BF16)
    lo = (r - mid.astype(F32)).astype(BF16)
    return hi, mid, lo


def _dot(a, b):
    return jnp.dot(a, b, preferred_element_type=F32)


def _dot_nt(a, b):
    return lax.dot_general(a, b, (((1,), (1,)), ((), ())), preferred_element_type=F32)


def _dot_tn(a, b):
    return lax.dot_general(a, b, (((0,), (0,)), ((), ())), preferred_element_type=F32)


def _dot3(a_hi, a_lo, b_hi, b_lo, dot=_dot):
    return dot(a_hi, b_hi) + dot(a_lo, b_hi) + dot(a_hi, b_lo)


def _dot_x3(a, b, dot=_dot):
    a_hi, a_lo = _split2(a)
    b_hi, b_lo = _split2(b)
    return _dot3(a_hi, a_lo, b_hi, b_lo, dot)


def _dot_exact_lhs(a_exact, b):
    a = a_exact.astype(BF16)
    b_hi, b_mid, b_lo = _split3(b)
    return _dot(a, b_hi) + _dot(a, b_mid) + _dot(a, b_lo)


def _silu(x):
    return x * jax.nn.sigmoid(x)


def _ada_kernel(c_ref, w_ref, b_ref, o_ref):
    a = _silu(c_ref[...])
    o_ref[...] = jnp.dot(a, w_ref[...], precision=HIGHEST, preferred_element_type=F32) + b_ref[...]


def _ada(c_rows, w, b):
    rows, d = c_rows.shape
    n = w.shape[1]
    tn = _pick(n, 512, LANES)
    return pl.pallas_call(
        _ada_kernel,
        grid=(n // tn,),
        in_specs=[pl.BlockSpec((rows, d), lambda j: (0, 0)),
                  pl.BlockSpec((d, tn), lambda j: (0, j)),
                  pl.BlockSpec((1, tn), lambda j: (0, j))],
        out_specs=pl.BlockSpec((rows, tn), lambda j: (0, j)),
        out_shape=jax.ShapeDtypeStruct((rows, n), F32),
        compiler_params=_params("parallel"),
        name="ada_mod",
    )(c_rows, w, b.reshape(1, n))


def _modulated_norm(x, g, sc, sh):
    ms = jnp.mean(x * x, axis=-1, keepdims=True)
    return (x * lax.rsqrt(ms + NORM_EPS) * g) * (1.0 + sc) + sh


def _nm_kernel(x_ref, g_ref, sc_ref, sh_ref, w_ref, o_ref, h_ref):
    @pl.when(pl.program_id(2) == 0)
    def _():
        h_ref[...] = _modulated_norm(x_ref[...], g_ref[...], sc_ref[...], sh_ref[...]).astype(BF16)

    o_ref[...] = _dot(h_ref[...], w_ref[...]).astype(o_ref.dtype)


def _nm3_kernel(x_ref, g_ref, sc_ref, sh_ref, whi_ref, wlo_ref, o_ref, hhi_ref, hlo_ref):
    @pl.when(pl.program_id(2) == 0)
    def _():
        hi, lo = _split2(_modulated_norm(x_ref[...], g_ref[...], sc_ref[...], sh_ref[...]))
        hhi_ref[...] = hi
        hlo_ref[...] = lo

    o_ref[...] = _dot3(hhi_ref[...], hlo_ref[...], whi_ref[...], wlo_ref[...]).astype(o_ref.dtype)


def _norm_matmul(x, g, sc, sh, w, three_pass=False, out_dtype=F32):
    b, l, d = x.shape
    n = w.shape[1]
    tm = _pick(l, 256 if three_pass else 512)
    tn = _pick(n, 512, LANES)
    x_spec = pl.BlockSpec((None, tm, d), lambda bi, i, j: (bi, i, 0))
    g_spec = pl.BlockSpec((1, d), lambda bi, i, j: (0, 0))
    m_spec = pl.BlockSpec((None, 1, d), lambda bi, i, j: (bi, 0, 0))
    w_spec = pl.BlockSpec((d, tn), lambda bi, i, j: (0, j))
    o_spec = pl.BlockSpec((None, tm, tn), lambda bi, i, j: (bi, i, j))
    if three_pass:
        w_hi, w_lo = _split2(w)
        kern, w_args, w_specs = _nm3_kernel, (w_hi, w_lo), [w_spec, w_spec]
        scratch = [pltpu.VMEM((tm, d), BF16), pltpu.VMEM((tm, d), BF16)]
    else:
        kern, w_args, w_specs = _nm_kernel, (w.astype(BF16),), [w_spec]
        scratch = [pltpu.VMEM((tm, d), BF16)]
    return pl.pallas_call(
        kern,
        grid=(b, l // tm, n // tn),
        in_specs=[x_spec, g_spec, m_spec, m_spec] + w_specs,
        out_specs=o_spec,
        out_shape=jax.ShapeDtypeStruct((b, l, n), out_dtype),
        scratch_shapes=scratch,
        compiler_params=_params("parallel", "parallel", "arbitrary"),
        name="norm_matmul3" if three_pass else "norm_matmul",
    )(x, g.reshape(1, d), sc, sh, *w_args)


def _resid_mm_kernel(z_ref, w_ref, x_ref, gt_ref, o_ref):
    o_ref[...] = x_ref[...] + gt_ref[...] * _dot(z_ref[...].astype(BF16), w_ref[...])


def _resid_matmul(z, w, x, gate):
    b, l, k = z.shape
    n = w.shape[1]
    tm = _pick(l, 512)
    tn = _pick(n, 512, LANES)
    return pl.pallas_call(
        _resid_mm_kernel,
        grid=(b, l // tm, n // tn),
        in_specs=[pl.BlockSpec((None, tm, k), lambda bi, i, j: (bi, i, 0)),
                  pl.BlockSpec((k, tn), lambda bi, i, j: (0, j)),
                  pl.BlockSpec((None, tm, tn), lambda bi, i, j: (bi, i, j)),
                  pl.BlockSpec((None, 1, tn), lambda bi, i, j: (bi, 0, j))],
        out_specs=pl.BlockSpec((None, tm, tn), lambda bi, i, j: (bi, i, j)),
        out_shape=jax.ShapeDtypeStruct((b, l, n), F32),
        compiler_params=_params("parallel", "parallel", "parallel"),
        name="resid_matmul",
    )(z, w.astype(BF16), x, gate)


def _conv3_kernel(x_ref, p_ref, n_ref, w_ref, o_ref, *, tm, gdn_d, head_scale):
    i = pl.program_id(1)
    last = pl.num_programs(1) - 1
    x = x_ref[...]
    rows = lax.broadcasted_iota(I32, (tm, 1), 0)
    prev_row = jnp.where(i > 0, p_ref[SUBLANES - 1:SUBLANES, :], 0.0)
    next_row = jnp.where(i < last, n_ref[0:1, :], 0.0)
    x_m = jnp.where(rows == 0, prev_row, pltpu.roll(x, 1, axis=0))
    x_p = jnp.where(rows == tm - 1, next_row, pltpu.roll(x, tm - 1, axis=0))
    w = w_ref[...]
    y = x_m * w[0:1, :] + x * w[1:2, :] + x_p * w[2:3, :]
    if not gdn_d:
        o_ref[...] = y
        return
    y = _silu(y)
    sec = (pl.program_id(2) * y.shape[1]) // gdn_d
    for hd in range(y.shape[1] // LANES):
        y_h = y[:, hd * LANES:(hd + 1) * LANES]
        nrm = y_h * lax.rsqrt(jnp.sum(y_h * y_h, axis=-1, keepdims=True) + NORM_EPS)
        nrm = nrm * jnp.where(sec == 0, head_scale, 1.0)
        o_ref[:, hd * LANES:(hd + 1) * LANES] = jnp.where(sec < 2, nrm, y_h)


def _conv3(x, w, n_cols, gdn_d=0):
    b, l, _ = x.shape
    tm = _pick(l, 512)
    tc = _pick(gdn_d if gdn_d else n_cols, 1024, LANES)
    r8 = tm // SUBLANES
    kern = functools.partial(_conv3_kernel, tm=tm, gdn_d=gdn_d, head_scale=float(LANES) ** -0.5)
    return pl.pallas_call(
        kern,
        grid=(b, l // tm, n_cols // tc),
        in_specs=[pl.BlockSpec((None, tm, tc), lambda bi, i, c: (bi, i, c)),
                  pl.BlockSpec((None, SUBLANES, tc), lambda bi, i, c: (bi, jnp.maximum(i * r8 - 1, 0), c)),
                  pl.BlockSpec((None, SUBLANES, tc),
                               lambda bi, i, c: (bi, jnp.minimum((i + 1) * r8, l // SUBLANES - 1), c)),
                  pl.BlockSpec((3, tc), lambda bi, i, c: (0, c))],
        out_specs=pl.BlockSpec((None, tm, tc), lambda bi, i, c: (bi, i, c)),
        out_shape=jax.ShapeDtypeStruct((b, l, n_cols), F32),
        compiler_params=_params("parallel", "parallel", "parallel"),
        name="conv3_gdn" if gdn_d else "conv3",
    )(x, x, x, w)


def _hyfilt_kernel(band_ref, w1_ref, b1_ref, w2_ref, b2_ref, sf_ref, w3_ref, b3_ref, ld_ref,
                   o_ref, ss_ref, hid_ref, *, tl, seq, n_band):
    i = pl.program_id(1)
    r = i * tl + lax.broadcasted_iota(I32, (tl, 1), 0)
    pos = jnp.where(r < seq, r, 2 * seq - r).astype(F32)
    t = pos / float(seq - 1)

    @pl.when(pl.program_id(0) == 0)
    def _():
        omega = (2.0 * math.pi) * pos / float(seq)
        ang = omega * band_ref[...]
        lane = lax.broadcasted_iota(I32, (tl, LANES), 1)
        feats = jnp.where(lane == 0, t,
                          jnp.where(lane <= n_band, jnp.cos(ang),
                                    jnp.where(lane <= 2 * n_band, -jnp.sin(ang), 0.0)))
        sf = sf_ref[...]
        h1 = jnp.sin(sf[0:1, :] * (jnp.dot(feats, w1_ref[...], precision=HIGHEST) + b1_ref[...]))
        hid_ref[i] = jnp.sin(sf[1:2, :] * (jnp.dot(h1, w2_ref[...], precision=HIGHEST) + b2_ref[...]))

    h = jnp.dot(hid_ref[i], w3_ref[...], precision=HIGHEST) + b3_ref[...]
    out = jnp.where(r == seq, 0.0, h * jnp.exp(-t * jnp.exp(ld_ref[...])))
    o_ref[...] = out

    @pl.when(i == 0)
    def _():
        ss_ref[...] = jnp.zeros_like(ss_ref)

    ss_ref[...] += jnp.sum(out * out, axis=0, keepdims=True)


def _hyena_filters(seq, w1, b1, w2, b2, w3, b3, sin_freq, log_decay):
    emb, hid = w1.shape
    n_band = (emb - 1) // 2
    n_order, n_dir, d_model = log_decay.shape
    assert n_dir == 2
    n_out = n_order * d_model
    bands = jnp.linspace(1e-4, n_band - 1, n_band, dtype=F32)
    band_row = jnp.zeros((1, LANES), F32).at[0, 1:1 + n_band].set(bands).at[0, 1 + n_band:1 + 2 * n_band].set(bands)
    w1p = jnp.zeros((LANES, hid), F32).at[:emb].set(w1)
    tl = _pick(seq, 512)
    tn = _pick(d_model, 1024, LANES)
    per_d = d_model // tn
    kern = functools.partial(_hyfilt_kernel, tl=tl, seq=seq, n_band=n_band)
    full = lambda shape: pl.BlockSpec(shape, lambda j, i: (0,) * len(shape))
    src_col = lambda j, i: (0, ((j // per_d) * 2 + (i * tl) // seq) * per_d + j % per_d)
    return pl.pallas_call(
        kern,
        grid=(n_out // tn, 2 * seq // tl),
        in_specs=[full((1, LANES)), full((LANES, hid)), full((1, hid)), full((hid, hid)), full((1, hid)),
                  full((2, hid)),
                  pl.BlockSpec((hid, tn), src_col),
                  pl.BlockSpec((1, tn), src_col),
                  pl.BlockSpec((1, tn), src_col)],
        out_specs=[pl.BlockSpec((tl, tn), lambda j, i: (i, j)),
                   pl.BlockSpec((1, tn), lambda j, i: (0, j))],
        out_shape=[jax.ShapeDtypeStruct((2 * seq, n_out), F32), jax.ShapeDtypeStruct((1, n_out), F32)],
        scratch_shapes=[pltpu.VMEM((2 * seq // tl, tl, hid), F32)],
        compiler_params=_params("arbitrary", "arbitrary"),
        name="hyena_filters",
    )(band_row, w1p, b1.reshape(1, hid), w2, b2.reshape(1, hid), sin_freq, w3, b3.reshape(1, -1),
      log_decay.reshape(1, -1))


def _lmm_kernel(m_ref, x_ref, o_ref, *, gb):
    for g in range(gb):
        o_ref[g] = _dot(m_ref[g], x_ref[g].astype(BF16)).astype(o_ref.dtype)


def _lmm_epi_kernel(m_ref, x_ref, z_ref, gt_ref, bias_ref, o_ref, *, gb):
    for g in range(gb):
        conv = _dot(m_ref[g], x_ref[g].astype(BF16))
        o_ref[g] = gt_ref[g] * (conv + z_ref[g] * bias_ref[...])


def _fft_stage1(mats, x, epilogue=None, out_dtype=F32):
    g_n, r_out, r_in = mats.shape
    d = x.shape[-1]
    gb = _pick(g_n, 4, 1)
    td = _pick(d, 1024, LANES)
    m_spec = pl.BlockSpec((gb, r_out, r_in), lambda g, j: (g, 0, 0))
    x_spec = pl.BlockSpec((gb, r_in, td), lambda g, j: (g, 0, j))
    o_spec = pl.BlockSpec((gb, r_out, td), lambda g, j: (g, 0, j))
    if epilogue is None:
        kern, extra, extra_specs = functools.partial(_lmm_kernel, gb=gb), (), []
    else:
        kern = functools.partial(_lmm_epi_kernel, gb=gb)
        extra = epilogue
        extra_specs = [o_spec, o_spec, pl.BlockSpec((1, td), lambda g, j: (0, j))]
    return pl.pallas_call(
        kern,
        grid=(g_n // gb, d // td),
        in_specs=[m_spec, x_spec] + extra_specs,
        out_specs=o_spec,
        out_shape=jax.ShapeDtypeStruct((g_n, r_out, d), out_dtype),
        compiler_params=_params("parallel", "parallel"),
        name="fft_stage1" if epilogue is None else "fft_stage1_inv",
    )(mats, x, *extra)


def _s2f_kernel(m_ref, x_ref, o_ref, *, gb):
    for g in range(gb):
        o_ref[g] = _dot(m_ref[...], x_ref[g].astype(BF16))


def _s2c_kernel(m_ref, i_ref, x_ref, f_ref, ss_ref, o_ref, *, gb, half):
    scale = lax.rsqrt(ss_ref[...] + NORM_EPS)
    for g in range(gb):
        spec = _dot(m_ref[...], x_ref[g].astype(BF16))
        f = f_ref[g]
        s_re, s_im = spec[:half], spec[half:]
        f_re, f_im = f[:half], f[half:]
        prod = jnp.concatenate([s_re * f_re - s_im * f_im, s_re * f_im + s_im * f_re], axis=0) * scale
        o_ref[g] = _dot(i_ref[...], prod.astype(BF16)).astype(o_ref.dtype)


def _fft_stage2_fwd(mat, x):
    r = mat.shape[0]
    g_n, _, d = x.shape
    gb = _pick(g_n, 4, 1)
    td = _pick(d, 1024, LANES)
    m_spec = pl.BlockSpec((r, r), lambda g, j: (0, 0))
    x_spec = pl.BlockSpec((gb, r, td), lambda g, j: (g, 0, j))
    return pl.pallas_call(
        functools.partial(_s2f_kernel, gb=gb),
        grid=(g_n // gb, d // td),
        in_specs=[m_spec, x_spec],
        out_specs=x_spec,
        out_shape=jax.ShapeDtypeStruct(x.shape, F32),
        compiler_params=_params("parallel", "parallel"),
        name="fft_stage2_fwd",
    )(mat, x)


def _fft_stage2_conv(mat, imat, x, filt_spec, sumsq):
    r = mat.shape[0]
    g_n, _, d = x.shape
    gb = _pick(g_n, 4, 1)
    td = _pick(d, 1024, LANES)
    m_spec = pl.BlockSpec((r, r), lambda g, j: (0, 0))
    x_spec = pl.BlockSpec((gb, r, td), lambda g, j: (g, 0, j))
    return pl.pallas_call(
        functools.partial(_s2c_kernel, gb=gb, half=r // 2),
        grid=(g_n // gb, d // td),
        in_specs=[m_spec, m_spec, x_spec, x_spec, pl.BlockSpec((1, td), lambda g, j: (0, j))],
        out_specs=x_spec,
        out_shape=jax.ShapeDtypeStruct(x.shape, BF16),
        compiler_params=_params("parallel", "parallel"),
        name="fft_stage2_conv",
    )(mat, imat, x, filt_spec, sumsq)


def _dft_tables(seq):
    n = 2 * seq
    n2 = FFT_N2
    n1 = n // n2
    n1h = n1 // 2
    k1 = jnp.arange(n1, dtype=I32)
    n2i = jnp.arange(n2, dtype=I32)
    n1i = jnp.arange(n1, dtype=I32)
    pos = n1i[None, :] * n2 + n2i[:, None]
    prod = (k1[None, :, None] * pos[:, None, :]) % n
    ang = prod.astype(F32) * (2.0 * math.pi / n)
    c, s = jnp.cos(ang), jnp.sin(ang)
    ch, sh = c[:, :, :n1h], s[:, :, :n1h]
    fwd_data = jnp.concatenate([jnp.concatenate([ch, sh], axis=2),
                                jnp.concatenate([-sh, ch], axis=2)], axis=1)
    fwd_real = jnp.concatenate([c, -s], axis=1)
    ct, st = jnp.swapaxes(ch, 1, 2), jnp.swapaxes(sh, 1, 2)
    inv_data = jnp.concatenate([jnp.concatenate([ct, -st], axis=2),
                                jnp.concatenate([st, ct], axis=2)], axis=1) * (1.0 / n)
    a2 = ((n2i[:, None] * n2i[None, :]) % n2).astype(F32) * (2.0 * math.pi / n2)
    c2, s2 = jnp.cos(a2), jnp.sin(a2)
    m2 = jnp.concatenate([jnp.concatenate([c2, s2], axis=1), jnp.concatenate([-s2, c2], axis=1)], axis=0)
    m2i = jnp.concatenate([jnp.concatenate([c2, -s2], axis=1), jnp.concatenate([s2, c2], axis=1)], axis=0)
    bf = lambda a: a.astype(BF16)
    return dict(n1=n1, n1h=n1h, fwd_data=bf(fwd_data), fwd_real=bf(fwd_real), inv_data=bf(inv_data),
                m2=bf(m2), m2i=bf(m2i))


def _to_stage2_layout(a, n1):
    n2, r, d = a.shape
    c = r // n1
    return a.reshape(n2, c, n1, d).transpose(2, 1, 0, 3).reshape(n1, c * n2, d)


def _to_stage1_layout(a, n2):
    n1, r, d = a.shape
    c = r // n2
    return a.reshape(n1, c, n2, d).transpose(2, 1, 0, 3).reshape(n2, c * n1, d)


def _hyena_core(u, two_sided, sumsq, bias):
    b, seq, d3 = u.shape
    d = d3 // 3
    n_order = bias.shape[0]
    assert b == 2
    tab = _dft_tables(seq)
    n1, n1h, n2 = tab["n1"], tab["n1h"], FFT_N2
    ut = u.reshape(b, n1h, n2, 3, d).transpose(3, 2, 0, 1, 4).reshape(3, n2, b * n1h, d)
    f_t = two_sided.reshape(n1, n2, n_order, d).transpose(2, 1, 0, 3)
    z_t = ut[0]
    for o in range(n_order):
        f_half = _fft_stage1(tab["fwd_real"], f_t[o], out_dtype=BF16)
        f_spec = _fft_stage2_fwd(tab["m2"], _to_stage2_layout(f_half, n1))
        ss = sumsq.reshape(n_order, 1, d)[o]
        a = _to_stage2_layout(_fft_stage1(tab["fwd_data"], z_t, out_dtype=BF16), n1)
        c = _to_stage1_layout(_fft_stage2_conv(tab["m2"], tab["m2i"], a, f_spec, ss), n2)
        z_t = _fft_stage1(tab["inv_data"], c, epilogue=(z_t, ut[1 + o], bias[o].reshape(1, d)))
    return z_t.reshape(n2, b, n1h, d).transpose(1, 2, 0, 3).reshape(b, seq, d)


def _hyena_mixer(x, g, sc, sh, gate, params):
    w_in, conv, w1, b1, w2, b2, w3, b3, sin_freq, log_decay, bias, w_out = params
    seq = x.shape[1]
    y = _norm_matmul(x, g, sc, sh, w_in)
    u = _conv3(y, conv, y.shape[-1])
    filt, sumsq = _hyena_filters(seq, w1, b1, w2, b2, w3, b3, sin_freq, log_decay)
    z = _hyena_core(u, filt, sumsq, bias)
    return _resid_matmul(z, w_out, x, gate)


def _pool_kernel(x_ref, p_ref, n_ref, o_ref, *, tm, seq, group):
    i = pl.program_id(1)
    last = pl.num_programs(1) - 1
    x = x_ref[...]
    prev = jnp.where(i > 0, p_ref[...], 0.0)
    nxt = jnp.where(i < last, n_ref[...], 0.0)
    ext = jnp.concatenate([prev, x, nxt], axis=0)
    gi = (pl.program_id(2) * x.shape[1]) // group
    half = jnp.left_shift(1, gi)
    r = lax.broadcasted_iota(I32, (tm, tm + 2 * POOL_HALO), 0)
    c = lax.broadcasted_iota(I32, (tm, tm + 2 * POOL_HALO), 1)
    band = jnp.logical_and(c >= r + POOL_HALO - half, c < r + POOL_HALO + half).astype(F32)
    win = _dot_exact_lhs(band, ext)
    t = i * tm + lax.broadcasted_iota(I32, (tm, 1), 0)
    cnt = (jnp.minimum(t + half, seq) - jnp.maximum(t - half, 0)).astype(F32)
    o_ref[...] = win / cnt - x


def _pool_windows(uf):
    b, l, d = uf.shape
    group = d // len(POOL_WINDOWS)
    tm = _pick(l, 256)
    tc = _pick(group, 512, LANES)
    r8 = tm // SUBLANES
    return pl.pallas_call(
        functools.partial(_pool_kernel, tm=tm, seq=l, group=group),
        grid=(b, l // tm, d // tc),
        in_specs=[pl.BlockSpec((None, tm, tc), lambda bi, i, c: (bi, i, c)),
                  pl.BlockSpec((None, SUBLANES, tc), lambda bi, i, c: (bi, jnp.maximum(i * r8 - 1, 0), c)),
                  pl.BlockSpec((None, SUBLANES, tc),
                               lambda bi, i, c: (bi, jnp.minimum((i + 1) * r8, l // SUBLANES - 1), c))],
        out_specs=pl.BlockSpec((None, tm, tc), lambda bi, i, c: (bi, i, c)),
        out_shape=jax.ShapeDtypeStruct((b, l, d), F32),
        compiler_params=_params("parallel", "parallel", "parallel"),
        name="pool_windows",
    )(uf, uf, uf)


def _group_mm_kernel(p_ref, w_ref, s_ref, o_ref):
    o_ref[...] = (_dot(p_ref[...].astype(BF16), w_ref[...]) * s_ref[...]).astype(o_ref.dtype)


def _group_matmul(p, w_grp, scale):
    b, l, d = p.shape
    n_g, gd, _ = w_grp.shape
    tm = _pick(l, 512)
    tn = _pick(gd, 512, LANES)
    per = gd // tn
    return pl.pallas_call(
        _group_mm_kernel,
        grid=(b, l // tm, n_g, per),
        in_specs=[pl.BlockSpec((None, tm, gd), lambda bi, i, g, j: (bi, i, g)),
                  pl.BlockSpec((None, gd, tn), lambda bi, i, g, j: (g, 0, j)),
                  pl.BlockSpec((1, tn), lambda bi, i, g, j: (0, g * per + j))],
        out_specs=pl.BlockSpec((None, tm, tn), lambda bi, i, g, j: (bi, i, g * per + j)),
        out_shape=jax.ShapeDtypeStruct((b, l, d), BF16),
        compiler_params=_params("parallel", "parallel", "parallel", "parallel"),
        name="pool_group_matmul",
    )(p, w_grp.astype(BF16), scale.reshape(1, d))


def _pool_mixer(x, g, sc, sh, gate, params):
    w_in, w_grp, scale, w_out = params
    uf = _norm_matmul(x, g, sc, sh, w_in)
    y = _group_matmul(_pool_windows(uf), w_grp, scale)
    return _resid_matmul(y, w_out, x, gate)


def _gdn_gates_kernel(x_ref, alog_ref, dtb_ref, o_ref, *, n_heads):
    x = x_ref[...]
    lane = lax.broadcasted_iota(I32, x.shape, 1)
    decay = -jnp.exp(alog_ref[...]) * jax.nn.softplus(x + dtb_ref[...])
    o_ref[...] = jnp.where(lane < 2 * n_heads, jax.nn.sigmoid(x), decay)


def _gdn_gates(proj, col_block, a_log, dt_bias):
    b, l, _ = proj.shape
    n_heads = a_log.shape[1]
    pad = jnp.zeros((1, 2 * n_heads), F32)
    alog_row = jnp.concatenate([pad, a_log.reshape(1, 2 * n_heads)], axis=1)
    dtb_row = jnp.concatenate([pad, dt_bias.reshape(1, 2 * n_heads)], axis=1)
    tm = _pick(l, 1024)
    w = 4 * n_heads
    return pl.pallas_call(
        functools.partial(_gdn_gates_kernel, n_heads=n_heads),
        grid=(b, l // tm),
        in_specs=[pl.BlockSpec((None, tm, w), lambda bi, i: (bi, i, col_block)),
                  pl.BlockSpec((1, w), lambda bi, i: (0, 0)),
                  pl.BlockSpec((1, w), lambda bi, i: (0, 0))],
        out_specs=pl.BlockSpec((None, tm, w), lambda bi, i: (bi, i, 0)),
        out_shape=jax.ShapeDtypeStruct((b, l, w), F32),
        compiler_params=_params("parallel", "parallel"),
        name="gdn_gates",
    )(proj, alog_row, dtb_row)


def _unit_triangular_inverse(a, row, col):
    n = a[0].shape[0]
    eye = (row == col).astype(F32)
    blk = lambda s: (row // s) == (col // s)
    a8 = [jnp.where(blk(8), x, 0.0) for x in a]
    a8_2 = _each(_dot_x3, a8, a8)
    a8_4 = _each(_dot_x3, a8_2, a8_2)
    t = _each(_dot_x3, [eye - x for x in a8], [eye + x for x in a8_2])
    t = _each(_dot_x3, t, [eye + x for x in a8_4])
    s = 8
    while s < n:
        mask = jnp.logical_and(blk(2 * s), jnp.logical_not(blk(s)))
        off = [jnp.where(mask, x, 0.0) for x in a]
        corr = _each(_dot_x3, _each(_dot_x3, t, off), t)
        t = [x - y for x, y in zip(t, corr)]
        s *= 2
    return t


def _each(fn, *lists):
    return [fn(*xs) for xs in zip(*lists)]


def _gdn_chunks(chains, head, row, col, lane):
    c_n = GDN_CHUNK
    q, k, v, gb, s_mat, rev, beta0, g0 = (list(z) for z in zip(*chains))
    incl = [(row <= col) if r else (row >= col) for r in rev]
    strict = [(row < col) if r else (row > col) for r in rev]
    edge = [0 if r else c_n - 1 for r in rev]
    eye = row == col
    ones = jnp.ones((c_n, c_n), F32)
    beta = [jnp.sum(jnp.where(lane == b0 + head, x, 0.0), axis=1, keepdims=True) for x, b0 in zip(gb, beta0)]
    g = [jnp.sum(jnp.where(lane == b0 + head, x, 0.0), axis=1, keepdims=True) for x, b0 in zip(gb, g0)]
    gc = [_dot_exact_lhs(m.astype(F32), jnp.broadcast_to(x, (c_n, LANES))) for m, x in zip(incl, g)]
    gc_col = [x[:, :c_n] for x in gc]
    gc_row = [_dot_exact_lhs(ones, jnp.where(eye, x, 0.0)) for x in gc_col]
    decay = [jnp.where(m, jnp.exp(jnp.where(m, c - r, 0.0)), 0.0) for m, c, r in zip(incl, gc_col, gc_row)]
    kb = [x * b for x, b in zip(k, beta)]
    vb = [x * b for x, b in zip(v, beta)]
    k16 = [x.astype(BF16) for x in k]
    kk = _each(_dot_nt, [x.astype(BF16) for x in kb], k16)
    a_kk = [jnp.where(m, x * d, 0.0) for m, x, d in zip(strict, kk, decay)]
    t_inv = _unit_triangular_inverse(a_kk, row, col)
    e_gc = [jnp.exp(x) for x in gc]
    rhs = [jnp.concatenate([x, y * e], axis=1) for x, y, e in zip(vb, kb, e_gc)]
    sol = _each(_dot_x3, t_inv, rhs)
    qk = _each(_dot_nt, [x.astype(BF16) for x in q], k16)
    a_qk = [(x * d).astype(BF16) for x, d in zip(qk, decay)]
    gc_edge = [x[e:e + 1, :] for x, e in zip(gc, edge)]
    q_s = [(x * e).astype(BF16) for x, e in zip(q, e_gc)]
    k_tail = [(x * jnp.exp(ge - c)).astype(BF16) for x, ge, c in zip(k, gc_edge, gc)]
    s16 = [x.astype(BF16) for x in s_mat]
    ws_s = _each(_dot, [x[:, LANES:].astype(BF16) for x in sol], s16)
    u16 = [(x[:, :LANES] - y).astype(BF16) for x, y in zip(sol, ws_s)]
    o_state = _each(_dot, q_s, s16)
    o_local = _each(_dot, a_qk, u16)
    s_add = _each(_dot_tn, k_tail, u16)
    o = [x + y for x, y in zip(o_state, o_local)]
    s_new = [x * jnp.exp(ge) + y for x, ge, y in zip(s_mat, gc_edge, s_add)]
    return o, s_new


def _gdn_scan_kernel(qf_ref, kf_ref, vf_ref, gf_ref, qb_ref, kb_ref, vb_ref, gb_ref, of_ref, ob_ref, s_ref,
                     *, n_chunks, n_batch, n_heads):
    head = pl.program_id(0) * GDN_HEADS_PER_STEP

    @pl.when(pl.program_id(1) == 0)
    def _():
        s_ref[...] = jnp.zeros_like(s_ref)

    c_n = GDN_CHUNK
    row = lax.broadcasted_iota(I32, (c_n, c_n), 0)
    col = lax.broadcasted_iota(I32, (c_n, c_n), 1)
    lane = lax.broadcasted_iota(I32, (c_n, LANES), 1)
    dirs = ((qf_ref, kf_ref, vf_ref, gf_ref, of_ref, False, 0, 2 * n_heads),
            (qb_ref, kb_ref, vb_ref, gb_ref, ob_ref, True, n_heads, 3 * n_heads))

    def chunk(ci, carry):
        chains, dests = [], []
        for di, (q_ref, k_ref, v_ref, g_ref, o_ref, rev, beta0, g0) in enumerate(dirs):
            cj = (n_chunks - 1 - ci) if rev else ci
            r0 = pl.multiple_of(cj * c_n, c_n)
            for bi in range(n_batch):
                gates = g_ref[bi, pl.ds(r0, c_n), :]
                for hh in range(GDN_HEADS_PER_STEP):
                    si = (di * n_batch + bi) * GDN_HEADS_PER_STEP + hh
                    cols = slice(hh * LANES, (hh + 1) * LANES)
                    chains.append((q_ref[bi, pl.ds(r0, c_n), cols], k_ref[bi, pl.ds(r0, c_n), cols],
                                   v_ref[bi, pl.ds(r0, c_n), cols], gates, s_ref[si],
                                   rev, beta0 + hh, g0 + hh))
                    dests.append((o_ref, bi, r0, cols, si))
        o, s_new = _gdn_chunks(chains, head, row, col, lane)
        for o_c, s_c, (o_ref, bi, r0, cols, si) in zip(o, s_new, dests):
            s_ref[si] = s_c
            o_ref[bi, pl.ds(r0, c_n), cols] = o_c
        return carry

    lax.fori_loop(0, n_chunks, chunk, 0)


def _gdn_scan(qkv, gb, d_model):
    b, l, _ = qkv.shape
    n_heads = d_model // LANES
    tl = _pick(l, 256, GDN_CHUNK)
    n_l = l // tl
    kern = functools.partial(_gdn_scan_kernel, n_chunks=tl // GDN_CHUNK, n_batch=b, n_heads=n_heads)
    hps = GDN_HEADS_PER_STEP
    assert n_heads % hps == 0
    n_hb = n_heads // hps
    wide = hps * LANES
    fwd = lambda sec: pl.BlockSpec((b, tl, wide), lambda h, i: (0, i, sec * n_hb + h))
    bwd = lambda sec: pl.BlockSpec((b, tl, wide), lambda h, i: (0, n_l - 1 - i, sec * n_hb + h))
    gate_f = pl.BlockSpec((b, tl, gb.shape[-1]), lambda h, i: (0, i, 0))
    gate_b = pl.BlockSpec((b, tl, gb.shape[-1]), lambda h, i: (0, n_l - 1 - i, 0))
    out_sd = jax.ShapeDtypeStruct((b, l, d_model), F32)
    return pl.pallas_call(
        kern,
        grid=(n_hb, n_l),
        in_specs=[fwd(0), fwd(1), fwd(2), gate_f, bwd(0), bwd(1), bwd(2), gate_b],
        out_specs=[pl.BlockSpec((b, tl, wide), lambda h, i: (0, i, h)),
                   pl.BlockSpec((b, tl, wide), lambda h, i: (0, n_l - 1 - i, h))],
        out_shape=[out_sd, out_sd],
        scratch_shapes=[pltpu.VMEM((2 * b * hps, LANES, LANES), F32)],
        compiler_params=_params("parallel", "arbitrary"),
        name="gdn_scan",
    )(qkv, qkv, qkv, gb, qkv, qkv, qkv, gb)


def _gdn_out_kernel(of_ref, ob_ref, gt_ref, w_ref, o_ref):
    o = of_ref[...] + ob_ref[...]
    ms = jnp.mean(o * o, axis=-1, keepdims=True)
    y = o * lax.rsqrt(ms + NORM_EPS) * w_ref[...]
    o_ref[...] = (y * _silu(gt_ref[...])).astype(o_ref.dtype)


def _gdn_out(o_f, o_b, proj, gate_block0, o_norm):
    b, l, d = o_f.shape
    tm = _pick(l, 1024)
    spec = pl.BlockSpec((None, tm, LANES), lambda bi, i, h: (bi, i, h))
    return pl.pallas_call(
        _gdn_out_kernel,
        grid=(b, l // tm, d // LANES),
        in_specs=[spec, spec,
                  pl.BlockSpec((None, tm, LANES), lambda bi, i, h: (bi, i, gate_block0 + h)),
                  pl.BlockSpec((1, LANES), lambda bi, i, h: (0, 0))],
        out_specs=spec,
        out_shape=jax.ShapeDtypeStruct((b, l, d), BF16),
        compiler_params=_params("parallel", "parallel", "parallel"),
        name="gdn_out",
    )(o_f, o_b, proj, o_norm.reshape(1, LANES))


def _gdn_mixer(x, g, sc, sh, gate, params):
    w_in, conv, a_log, dt_bias, o_norm, w_out = params
    d = x.shape[-1]
    n_heads = a_log.shape[1]
    assert d // n_heads == LANES and 4 * n_heads == LANES
    proj = _norm_matmul(x, g, sc, sh, w_in)
    qkv = _conv3(proj, conv, 3 * d, gdn_d=d)
    gb = _gdn_gates(proj, (4 * d) // LANES, a_log, dt_bias)
    o_f, o_b = _gdn_scan(qkv, gb, d)
    y = _gdn_out(o_f, o_b, proj, (3 * d) // LANES, o_norm)
    return _resid_matmul(y, w_out, x, gate)


def _top_rows(s, k, payload=None):
    r_n = s.shape[0]
    row = lax.broadcasted_iota(I32, s.shape, 0)
    vals, idxs, pays = [], [], []
    for _ in range(k):
        m = jnp.max(s, axis=0, keepdims=True)
        idx = jnp.min(jnp.where(s == m, row, r_n), axis=0, keepdims=True)
        hit = row == idx
        vals.append(m)
        idxs.append(idx)
        if payload is not None:
            pays.append(jnp.max(jnp.where(hit, payload, -1), axis=0, keepdims=True))
        s = jnp.where(hit, -jnp.inf, s)
    out = (jnp.concatenate(vals, axis=0), jnp.concatenate(idxs, axis=0))
    if payload is not None:
        out += (jnp.concatenate(pays, axis=0),)
    return out


def _peer_route_kernel(q_ref, khi_ref, klo_ref, ids_ref, gates_ref, *, n_heads, n_keys, key_dim):
    k_top = PEER_TOPK
    for hd in range(n_heads):
        tops = []
        for p in range(2):
            c0 = (hd * 2 + p) * key_dim
            q_hi, q_lo = _split2(q_ref[:, c0:c0 + key_dim])
            s = _dot3(khi_ref[p], klo_ref[p], q_hi, q_lo, _dot_nt)
            tops.append(_top_rows(s, k_top))
        (s0, i0), (s1, i1) = tops
        keep = [min(k_top, -(-(k_top // (a + 1)) // SUBLANES) * SUBLANES) for a in range(k_top)]
        cand = jnp.concatenate([s0[a:a + 1, :] + s1[:keep[a]] for a in range(k_top)], axis=0)
        cid = jnp.concatenate([i0[a:a + 1, :] * n_keys + i1[:keep[a]] for a in range(k_top)], axis=0)
        best, _, ids = _top_rows(cand, k_top, payload=cid)
        e = jnp.exp(best - best[0:1, :])
        gates = e / jnp.sum(e, axis=0, keepdims=True)
        ids_ref[hd * k_top:(hd + 1) * k_top, :] = ids
        gates_ref[hd * k_top:(hd + 1) * k_top, :] = gates


def _peer_route(q, keys):
    t_n, qd = q.shape
    _, n_keys, key_dim = keys.shape
    n_heads = qd // (2 * key_dim)
    tm = _pick(t_n, 256, LANES)
    k_hi, k_lo = _split2(keys)
    kern = functools.partial(_peer_route_kernel, n_heads=n_heads, n_keys=n_keys, key_dim=key_dim)
    rows = n_heads * PEER_TOPK
    k_spec = pl.BlockSpec((2, n_keys, key_dim), lambda i: (0, 0, 0))
    o_spec = pl.BlockSpec((rows, tm), lambda i: (0, i))
    return pl.pallas_call(
        kern,
        grid=(t_n // tm,),
        in_specs=[pl.BlockSpec((tm, qd), lambda i: (i, 0)), k_spec, k_spec],
        out_specs=[o_spec, o_spec],
        out_shape=[jax.ShapeDtypeStruct((rows, t_n), I32), jax.ShapeDtypeStruct((rows, t_n), F32)],
        compiler_params=_params("parallel"),
        name="peer_route",
    )(q, k_hi, k_lo)


def _peer_gather_kernel(ids_ref, gates_ref, x_ref, g_ref, sc_ref, sh_ref, og_ref, uv_hbm,
                        o_ref, h_ref, buf, sem, *, tb, n_sel, n_slot, rows, grp):
    x = x_ref[...]
    ms = jnp.sum(jnp.sum(x * x, axis=2, keepdims=True), axis=1, keepdims=True) * (1.0 / (rows * LANES))
    h_ref[...] = (x * lax.rsqrt(ms + NORM_EPS) * g_ref[...]) * (1.0 + sc_ref[...]) + sh_ref[...]

    n_q = rows // grp
    jc_n = SUBLANES
    sel = (lax.broadcasted_iota(I32, (n_sel, n_sel * grp), 1) // grp
           == lax.broadcasted_iota(I32, (n_sel, n_sel * grp), 0)).astype(BF16)
    sel_t = (lax.broadcasted_iota(I32, (n_sel * grp, n_sel), 0) // grp
             == lax.broadcasted_iota(I32, (n_sel * grp, n_sel), 1)).astype(BF16)
    lane_t = lax.broadcasted_iota(I32, (n_sel, tb), 1)

    n_chunk = n_sel // jc_n

    def copies(t, slot):
        def start(j):
            pltpu.make_async_copy(uv_hbm.at[ids_ref[t, j]], buf.at[slot, j], sem.at[slot]).start()
        return [functools.partial(start, j) for j in range(n_sel)]

    def wait(slot):
        pltpu.make_async_copy(uv_hbm.at[pl.ds(0, n_sel)], buf.at[slot], sem.at[slot]).wait()

    def scores(t, slot, tick):
        h = h_ref[t]
        h_q = [h[qi * grp:(qi + 1) * grp] for qi in range(n_q)]
        parts = []
        for jc in range(n_chunk):
            ub = buf[slot, jc * jc_n:(jc + 1) * jc_n, 0:rows, :].astype(F32)
            q = ub[:, 0:grp] * h_q[0]
            for qi in range(1, n_q):
                q = q + ub[:, qi * grp:(qi + 1) * grp] * h_q[qi]
            parts.append(q)
            tick()
        q2 = jnp.concatenate(parts, axis=0).reshape(n_sel * grp, LANES)
        part = _dot(sel, q2.astype(BF16))
        act = jnp.sum(part, axis=1, keepdims=True)
        gate_col = jnp.sum(jnp.where(lane_t == t, gates_ref[...], 0.0), axis=1, keepdims=True)
        wgt = gate_col * (0.5 * act * (1.0 + lax.erf(act * (2.0 ** -0.5))))
        w_b = jnp.broadcast_to(wgt, (n_sel, LANES)).astype(BF16)
        return _dot(sel_t, w_b).reshape(n_sel, grp, LANES)

    def combine(t, slot, w_rep, tick):
        acc = [None] * n_q
        for jc in range(n_chunk):
            vb = buf[slot, jc * jc_n:(jc + 1) * jc_n, rows:2 * rows, :].astype(F32)
            ws = w_rep[jc * jc_n:(jc + 1) * jc_n]
            for qi in range(n_q):
                term = jnp.sum(vb[:, qi * grp:(qi + 1) * grp] * ws, axis=0)
                acc[qi] = term if acc[qi] is None else acc[qi] + term
            tick()
        y = jnp.concatenate(acc, axis=0) if n_q > 1 else acc[0]
        return x_ref[t] + og_ref[...] * y

    def pair(p, slot_pair, prefetch):
        ta, tb_ = 2 * p, 2 * p + 1
        sa, sb = 2 * slot_pair, 2 * slot_pair + 1
        pending = []
        if prefetch:
            nxt_pair = (slot_pair + n_pair - 1) % n_pair
            tn = 2 * (p + n_pair - 1)
            pending = copies(tn, 2 * nxt_pair) + copies(tn + 1, 2 * nxt_pair + 1)
        per_tick = -(-len(pending) // (4 * n_chunk))

        def tick():
            for _ in range(min(per_tick, len(pending))):
                pending.pop(0)()

        wait(sa)
        wait(sb)
        w_a = scores(ta, sa, tick)
        w_b = scores(tb_, sb, tick)
        out_a = combine(ta, sa, w_a, tick)
        out_b = combine(tb_, sb, w_b, tick)
        assert not pending
        o_ref[ta] = out_a
        o_ref[tb_] = out_b

    n_pair = n_slot // 2
    n_group = (tb // 2) // n_pair
    for sp in range(n_pair - 1):
        for start in copies(2 * sp, 2 * sp) + copies(2 * sp + 1, 2 * sp + 1):
            start()

    def group(gi, carry):
        for sp in range(n_pair):
            pair(gi * n_pair + sp, sp, True)
        return carry

    lax.fori_loop(0, n_group - 1, group, 0)
    for sp in range(n_pair):
        pair((n_group - 1) * n_pair + sp, sp, sp == 0)


def _peer_gather(ids, gates_t, x, g, sc, sh, gate, u_tab, v_tab):
    b, l, d = x.shape
    t_n, n_sel = ids.shape
    rows = d // LANES
    grp = min(SUBLANES, rows)
    tb = _pick(l, PEER_TOKENS_PER_STEP, LANES)
    assert tb % PEER_SLOTS == 0 and PEER_SLOTS >= 4 and n_sel % SUBLANES == 0
    per_b = l // tb
    tiles = lambda a: a.reshape(a.shape[0], rows, LANES)
    uv = jnp.concatenate([tiles(u_tab.astype(BF16)), tiles(v_tab.astype(BF16))], axis=1)
    mod_spec = pl.BlockSpec((None, rows, LANES), lambda i: (i // per_b, 0, 0))
    kern = functools.partial(_peer_gather_kernel, tb=tb, n_sel=n_sel, n_slot=PEER_SLOTS, rows=rows, grp=grp)
    out = pl.pallas_call(
        kern,
        grid=(t_n // tb,),
        in_specs=[pl.BlockSpec((tb, n_sel), lambda i: (i, 0), memory_space=pltpu.SMEM),
                  pl.BlockSpec((n_sel, tb), lambda i: (0, i)),
                  pl.BlockSpec((tb, rows, LANES), lambda i: (i, 0, 0)),
                  pl.BlockSpec((rows, LANES), lambda i: (0, 0)),
                  mod_spec, mod_spec, mod_spec,
                  pl.BlockSpec(memory_space=pl.ANY)],
        out_specs=pl.BlockSpec((tb, rows, LANES), lambda i: (i, 0, 0)),
        out_shape=jax.ShapeDtypeStruct((t_n, rows, LANES), F32),
        scratch_shapes=[pltpu.VMEM((tb, rows, LANES), F32),
                        pltpu.VMEM((PEER_SLOTS, n_sel, 2 * rows, LANES), BF16),
                        pltpu.SemaphoreType.DMA((PEER_SLOTS,))],
        compiler_params=_params("arbitrary"),
        name="peer_gather",
    )(ids, gates_t, x.reshape(t_n, rows, LANES), g.reshape(rows, LANES), tiles(sc), tiles(sh), tiles(gate), uv)
    return out.reshape(b, l, d)


def _peer(x, g, sc, sh, gate, params):
    w_q, keys, u_tab, v_tab = params
    b, l, d = x.shape
    q = _norm_matmul(x, g, sc, sh, w_q, three_pass=True)
    ids_t, gates_t = _peer_route(q.reshape(b * l, -1), keys)
    return _peer_gather(ids_t.T, gates_t, x, g, sc, sh, gate, u_tab, v_tab)


def _final_norm_kernel(x_ref, g_ref, o_ref):
    x = x_ref[...]
    ms = jnp.mean(x * x, axis=-1, keepdims=True)
    o_ref[...] = x * lax.rsqrt(ms + NORM_EPS) * g_ref[...]


def _final_norm(x, g):
    b, l, d = x.shape
    tm = _pick(l, 256)
    return pl.pallas_call(
        _final_norm_kernel,
        grid=(b, l // tm),
        in_specs=[pl.BlockSpec((None, tm, d), lambda bi, i: (bi, i, 0)),
                  pl.BlockSpec((1, d), lambda bi, i: (0, 0))],
        out_specs=pl.BlockSpec((None, tm, d), lambda bi, i: (bi, i, 0)),
        out_shape=jax.ShapeDtypeStruct((b, l, d), F32),
        compiler_params=_params("parallel", "parallel"),
        name="final_norm",
    )(x, g.reshape(1, d))


_MIXERS = (_hyena_mixer, _pool_mixer, _gdn_mixer)


def _trunk(x, mods, layers, final_norm):
    d = x.shape[-1]
    for li, (norm_tok, norm_ch, mixer_params, peer_params) in enumerate(layers):
        mod = mods[li].reshape(x.shape[0], 1, N_MOD, d)
        sh_t, sc_t, g_t, sh_c, sc_c, g_c = (mod[:, :, m] for m in range(N_MOD))
        x = _MIXERS[li % len(_MIXERS)](x, norm_tok, sc_t, sh_t, g_t, mixer_params)
        x = _peer(x, norm_ch, sc_c, sh_c, g_c, peer_params)
    return _final_norm(x, final_norm)


def kernel(x_prompt, x_sample, c_prompt, c_sample, l0_ada_w, l0_ada_b, l0_norm_tok, l0_norm_ch, l0_hy_w_in, l0_hy_conv, l0_hy_ffn_w1, l0_hy_ffn_b1, l0_hy_ffn_w2, l0_hy_ffn_b2, l0_hy_ffn_w3, l0_hy_ffn_b3, l0_hy_sin_freq, l0_hy_log_decay, l0_hy_bias, l0_hy_w_out, l0_peer_w_q, l0_peer_keys, l0_peer_u, l0_peer_v, l1_ada_w, l1_ada_b, l1_norm_tok, l1_norm_ch, l1_pool_w_in, l1_pool_w_grp, l1_pool_scale, l1_pool_w_out, l1_peer_w_q, l1_peer_keys, l1_peer_u, l1_peer_v, l2_ada_w, l2_ada_b, l2_norm_tok, l2_norm_ch, l2_gdn_w_in, l2_gdn_conv, l2_gdn_A_log, l2_gdn_dt_bias, l2_gdn_o_norm, l2_gdn_w_out, l2_peer_w_q, l2_peer_keys, l2_peer_u, l2_peer_v, l3_ada_w, l3_ada_b, l3_norm_tok, l3_norm_ch, l3_hy_w_in, l3_hy_conv, l3_hy_ffn_w1, l3_hy_ffn_b1, l3_hy_ffn_w2, l3_hy_ffn_b2, l3_hy_ffn_w3, l3_hy_ffn_b3, l3_hy_sin_freq, l3_hy_log_decay, l3_hy_bias, l3_hy_w_out, l3_peer_w_q, l3_peer_keys, l3_peer_u, l3_peer_v, final_norm):
    layers = (
        (l0_norm_tok, l0_norm_ch,
         (l0_hy_w_in, l0_hy_conv, l0_hy_ffn_w1, l0_hy_ffn_b1, l0_hy_ffn_w2, l0_hy_ffn_b2,
          l0_hy_ffn_w3, l0_hy_ffn_b3, l0_hy_sin_freq, l0_hy_log_decay, l0_hy_bias, l0_hy_w_out),
         (l0_peer_w_q, l0_peer_keys, l0_peer_u, l0_peer_v)),
        (l1_norm_tok, l1_norm_ch,
         (l1_pool_w_in, l1_pool_w_grp, l1_pool_scale, l1_pool_w_out),
         (l1_peer_w_q, l1_peer_keys, l1_peer_u, l1_peer_v)),
        (l2_norm_tok, l2_norm_ch,
         (l2_gdn_w_in, l2_gdn_conv, l2_gdn_A_log, l2_gdn_dt_bias, l2_gdn_o_norm, l2_gdn_w_out),
         (l2_peer_w_q, l2_peer_keys, l2_peer_u, l2_peer_v)),
        (l3_norm_tok, l3_norm_ch,
         (l3_hy_w_in, l3_hy_conv, l3_hy_ffn_w1, l3_hy_ffn_b1, l3_hy_ffn_w2, l3_hy_ffn_b2,
          l3_hy_ffn_w3, l3_hy_ffn_b3, l3_hy_sin_freq, l3_hy_log_decay, l3_hy_bias, l3_hy_w_out),
         (l3_peer_w_q, l3_peer_keys, l3_peer_u, l3_peer_v)),
    )
    ada = ((l0_ada_w, l0_ada_b), (l1_ada_w, l1_ada_b), (l2_ada_w, l2_ada_b), (l3_ada_w, l3_ada_b))
    n_p = c_prompt.shape[0]
    n_s = c_sample.shape[0]
    pad = (-(n_p + n_s)) % SUBLANES
    c_rows = jnp.concatenate([c_prompt, c_sample, jnp.zeros((pad, c_prompt.shape[1]), F32)], axis=0)
    mods = [_ada(c_rows, w, b) for w, b in ada]
    y_prompt = _trunk(x_prompt, [m[:n_p] for m in mods], layers, final_norm)
    y_sample = _trunk(x_sample, [m[n_p:n_p + n_s] for m in mods], layers, final_norm)
    return (y_prompt, y_sample)
```
